```python
import math
import jax, jax.numpy as jnp
from jax import lax
import numpy as np

D_MODEL = 1024
BATCH = 16
SEQ = 2048
DEPTH = 2
DEC_BATCH = 8
DEC_SEQ = 16
PAST_LEN = 1024

CHUNK = 64
Q_BLOCK = 128
N_EVEN = (DEPTH + 1) // 2
N_ODD = DEPTH // 2
EPS = 1e-5
ROPE_THETA = 10000.0
N_MOD = 9
D_FF = ((8 * D_MODEL // 3 + 127) // 128) * 128
A_HEAD_DIM = 64
A_HEADS = D_MODEL // (4 * A_HEAD_DIM)
A_V_DIM = 2 * A_HEAD_DIM
A_SCALE = A_HEAD_DIM ** -0.5
MLA_HEADS = D_MODEL // 256
MLA_NOPE = 128
MLA_ROPE = 64
MLA_V = 128
MLA_Q_RANK = 3 * D_MODEL // 8
MLA_KV_RANK = D_MODEL // 4
MLA_SCALE = (MLA_NOPE + MLA_ROPE) ** -0.5
AB_IN = 2 * A_HEADS * 2 * A_HEAD_DIM + A_HEADS * A_V_DIM + MLA_Q_RANK + MLA_KV_RANK + MLA_ROPE
AB_OUT = A_HEADS * A_V_DIM + MLA_HEADS * MLA_V
C_HEAD_DIM = 64
C_HEADS = D_MODEL // C_HEAD_DIM
C_SCALE = C_HEAD_DIM ** -0.5
C_PAST_CHUNKS = 8
C_BAND_PAST = C_PAST_CHUNKS * CHUNK
C_BAND = C_BAND_PAST + CHUNK
REL_CLIP = 128

kernel_name = 'hybrid_streaming_encoder_step'


def _rmsnorm(x, g):
    xf = x.astype(jnp.float32)
    y = xf * lax.rsqrt(jnp.mean(xf * xf, axis=-1, keepdims=True) + EPS) * g.astype(jnp.float32)
    return y.astype(x.dtype)


def _modulate(x, g, shift, scale):
    return _rmsnorm(x, g) * (1 + scale[:, None, :]) + shift[:, None, :]


def _swiglu(h, w_in, w_out):
    gate, up = jnp.split(h @ w_in, 2, axis=-1)
    return (jax.nn.silu(gate) * up) @ w_out


def _rope(x, pos):
    d = x.shape[-1]
    half = d // 2
    inv = 1.0 / (ROPE_THETA ** (jnp.arange(half, dtype=jnp.float32) * (2.0 / d)))
    ang = pos.astype(jnp.float32)[:, None] * inv[None, :]
    shape = (1, pos.shape[0]) + (1,) * (x.ndim - 3) + (half,)
    cos = jnp.cos(ang).reshape(shape).astype(x.dtype)
    sin = jnp.sin(ang).reshape(shape).astype(x.dtype)
    x1, x2 = x[..., :half], x[..., half:]
    return jnp.concatenate([x1 * cos - x2 * sin, x2 * cos + x1 * sin], axis=-1)


def _sweep_query_blocks(fn, q_pos, *qs):
    L = q_pos.shape[0]
    qb = min(Q_BLOCK, L)
    nb = L // qb
    def split(a):
        return jnp.swapaxes(a.reshape((a.shape[0], nb, qb) + a.shape[2:]), 0, 1)
    outs = lax.map(fn, (q_pos.reshape(nb, qb),) + tuple(split(a) for a in qs))
    def merge(o):
        return jnp.swapaxes(o, 0, 1).reshape((o.shape[1], L) + o.shape[3:])
    return jax.tree_util.tree_map(merge, outs)


def _ab_mixer(h, pos0, past, lam_init, w_in, a_lambda, a_subln_g, q_norm_g, w_uq,
              kv_norm_g, w_ukv, w_out):
    bsz, L, _ = h.shape
    q_pos = pos0 + jnp.arange(L, dtype=jnp.int32)
    a_qk = A_HEADS * 2 * A_HEAD_DIM
    cuts = np.cumsum([a_qk, a_qk, A_HEADS * A_V_DIM, MLA_Q_RANK, MLA_KV_RANK]).tolist()
    aq, ak, av, cq, ckv, kr = jnp.split(h @ w_in, cuts, axis=-1)
    aq = _rope(aq.reshape(bsz, L, A_HEADS, 2, A_HEAD_DIM), q_pos)
    ak = _rope(ak.reshape(bsz, L, A_HEADS, 2, A_HEAD_DIM), q_pos)
    av = av.reshape(bsz, L, A_HEADS, A_V_DIM)
    bq = (_rmsnorm(cq, q_norm_g) @ w_uq).reshape(bsz, L, MLA_HEADS, MLA_NOPE + MLA_ROPE)
    bq_nope = bq[..., :MLA_NOPE]
    bq_rope = _rope(bq[..., MLA_NOPE:], q_pos)
    lat = _rmsnorm(ckv, kv_norm_g)
    kr = _rope(kr, q_pos)
    new_rows = (ak, av, lat, kr)
    if past is not None:
        ak, av, lat, kr = (jnp.concatenate([p_, n_], axis=1) for p_, n_ in zip(past, new_rows))
    Lk = ak.shape[1]
    k_pos = jnp.arange(Lk, dtype=jnp.int32)
    kv = (lat @ w_ukv).reshape(bsz, Lk, MLA_HEADS, MLA_NOPE + MLA_V)
    bk_nope, bv = kv[..., :MLA_NOPE], kv[..., MLA_NOPE:]
    lf = a_lambda.astype(jnp.float32)
    lam = jnp.exp(jnp.sum(lf[0] * lf[1])) - jnp.exp(jnp.sum(lf[2] * lf[3])) + lam_init

    def attend_block(args):
        qp, q_a, q_bn, q_br = args
        visible = (k_pos[None, :] // CHUNK) <= (qp[:, None] // CHUNK)
        s_a = jnp.einsum('bqhtd,bkhtd->bhtqk', q_a, ak).astype(jnp.float32) * A_SCALE
        p_a = jax.nn.softmax(jnp.where(visible, s_a, -jnp.inf), axis=-1)
        w_a = (p_a[:, :, 0] - lam * p_a[:, :, 1]).astype(av.dtype)
        o_a = jnp.einsum('bhqk,bkhe->bqhe', w_a, av)
        s_b = (jnp.einsum('bqhd,bkhd->bhqk', q_bn, bk_nope)
               + jnp.einsum('bqhr,bkr->bhqk', q_br, kr)).astype(jnp.float32) * MLA_SCALE
        p_b = jax.nn.softmax(jnp.where(visible, s_b, -jnp.inf), axis=-1).astype(bv.dtype)
        o_b = jnp.einsum('bhqk,bkhe->bqhe', p_b, bv)
        return (o_a, o_b)

    o_a, o_b = _sweep_query_blocks(attend_block, q_pos, aq, bq_nope, bq_rope)
    o_a = _rmsnorm(o_a, a_subln_g) * (1.0 - lam_init)
    out = jnp.concatenate([o_a.reshape(bsz, L, -1), o_b.reshape(bsz, L, -1)], axis=-1) @ w_out
    return out, new_rows


def _band_attend(q, k, v, q_pos, k_pos, rel_bias):
    qc = q_pos[:, None] // CHUNK
    kc = k_pos[None, :] // CHUNK
    valid = (kc <= qc) & (kc >= qc - C_PAST_CHUNKS) & (k_pos[None, :] >= 0)
    rel = jnp.clip(k_pos[None, :] - q_pos[:, None], -REL_CLIP, REL_CLIP) + REL_CLIP
    bias = rel_bias[:, rel].astype(jnp.float32)
    s = jnp.einsum('bqhd,bkhd->bhqk', q, k).astype(jnp.float32) * C_SCALE + bias
    p = jax.nn.softmax(jnp.where(valid, s, -jnp.inf), axis=-1).astype(v.dtype)
    return jnp.einsum('bhqk,bkhd->bqhd', p, v)


def _c_mixer(h, pos0, past, w_in, rel_bias, w_out):
    bsz, L, _ = h.shape
    q, k, v = (t.reshape(bsz, L, C_HEADS, C_HEAD_DIM) for t in jnp.split(h @ w_in, 3, axis=-1))
    q_pos = pos0 + jnp.arange(L, dtype=jnp.int32)
    if past is None:
        pad = ((0, 0), (C_BAND_PAST, 0), (0, 0), (0, 0))
        k_pad, v_pad = jnp.pad(k, pad), jnp.pad(v, pad)
        n_chunks = L // CHUNK
        q_chunks = jnp.swapaxes(q.reshape(bsz, n_chunks, CHUNK, C_HEADS, C_HEAD_DIM), 0, 1)

        def chunk_step(args):
            ci, q_c = args
            start = ci * CHUNK
            kb = lax.dynamic_slice_in_dim(k_pad, start, C_BAND, axis=1)
            vb = lax.dynamic_slice_in_dim(v_pad, start, C_BAND, axis=1)
            qp = start + jnp.arange(CHUNK, dtype=jnp.int32)
            kp = start - C_BAND_PAST + jnp.arange(C_BAND, dtype=jnp.int32)
            return _band_attend(q_c, kb, vb, qp, kp, rel_bias)

        o = lax.map(chunk_step, (jnp.arange(n_chunks, dtype=jnp.int32), q_chunks))
        o = jnp.swapaxes(o, 0, 1).reshape(bsz, L, C_HEADS * C_HEAD_DIM)
        keep = min(C_BAND_PAST, L)
        new_state = (k[:, L - keep:], v[:, L - keep:])
    else:
        pk, pv = past
        lc = pk.shape[1]
        k_all = jnp.concatenate([pk, k], axis=1)
        v_all = jnp.concatenate([pv, v], axis=1)
        k_pos = jnp.concatenate([pos0 - lc + jnp.arange(lc, dtype=jnp.int32), q_pos])
        o = _band_attend(q, k_all, v_all, q_pos, k_pos, rel_bias).reshape(bsz, L, C_HEADS * C_HEAD_DIM)
        new_state = (k_all[:, -lc:], v_all[:, -lc:])
    return o @ w_out, new_state


def _trunk(x, c, past, prm):
    pos0 = 0 if past is None else past['a_k'].shape[2]
    ab_rows, c_rows = [], []
    sc = jax.nn.silu(c)
    for l in range(DEPTH):
        mod = (sc @ prm['ada_w'][l] + prm['ada_b'][l]).reshape(c.shape[0], N_MOD, D_MODEL)
        g = prm['norm_g'][l]
        x = x + 0.5 * mod[:, 2][:, None] * _swiglu(_modulate(x, g[0], mod[:, 0], mod[:, 1]),
                                                    prm['ffn_w_in'][l, 0], prm['ffn_w_out'][l, 0])
        h = _modulate(x, g[1], mod[:, 3], mod[:, 4])
        if l % 2 == 0:
            i = l // 2
            lp = None if past is None else (past['a_k'][i], past['a_v'][i],
                                            past['mla_latent'][i], past['mla_krope'][i])
            mix, rows = _ab_mixer(h, pos0, lp, 0.8 - 0.6 * math.exp(-0.3 * l),
                                  prm['ab_w_in'][i], prm['a_lambda'][i], prm['a_subln_g'][i],
                                  prm['mla_q_norm_g'][i], prm['mla_w_uq'][i],
                                  prm['mla_kv_norm_g'][i], prm['mla_w_ukv'][i], prm['ab_w_out'][i])
            ab_rows.append(rows)
        else:
            i = l // 2
            lp = None if past is None else (past['c_k'][i], past['c_v'][i])
            mix, rows = _c_mixer(h, pos0, lp, prm['c_w_in'][i], prm['c_rel_bias'][i], prm['c_w_out'][i])
            c_rows.append(rows)
        x = x + mod[:, 5][:, None] * mix
        x = x + 0.5 * mod[:, 8][:, None] * _swiglu(_modulate(x, g[2], mod[:, 6], mod[:, 7]),
                                                    prm['ffn_w_in'][l, 1], prm['ffn_w_out'][l, 1])
    y = _rmsnorm(x, prm['final_norm_g'])
    ab_state = tuple(jnp.stack([r[j] for r in ab_rows]) for j in range(4))
    c_state = tuple(jnp.stack([r[j] for r in c_rows]) for j in range(2))
    return y, ab_state, c_state


def setup_inputs(seed: int = 0) -> dict:
    key = jax.random.key(seed)
    k = jax.random.split(key, 27)
    f32 = jnp.float32
    def nrm(kk, shape, scale):
        return jax.random.normal(kk, shape, f32) * scale
    def gain(kk, shape):
        return 1.0 + 0.02 * jax.random.normal(kk, shape, f32)
    c_len = min(C_BAND_PAST, PAST_LEN)
    return {
        'x_prompt': nrm(k[0], (BATCH, SEQ, D_MODEL), 1.0),
        'x_sample': nrm(k[1], (DEC_BATCH, DEC_SEQ, D_MODEL), 1.0),
        'cache_a_k': nrm(k[2], (N_EVEN, DEC_BATCH, PAST_LEN, A_HEADS, 2, A_HEAD_DIM), 1.0),
        'cache_a_v': nrm(k[3], (N_EVEN, DEC_BATCH, PAST_LEN, A_HEADS, A_V_DIM), 1.0),
        'cache_mla_latent': nrm(k[4], (N_EVEN, DEC_BATCH, PAST_LEN, MLA_KV_RANK), 1.0),
        'cache_mla_krope': nrm(k[5], (N_EVEN, DEC_BATCH, PAST_LEN, MLA_ROPE), 1.0),
        'cache_c_k': nrm(k[6], (N_ODD, DEC_BATCH, c_len, C_HEADS, C_HEAD_DIM), 1.0),
        'cache_c_v': nrm(k[7], (N_ODD, DEC_BATCH, c_len, C_HEADS, C_HEAD_DIM), 1.0),
        'c_prompt': nrm(k[8], (BATCH, D_MODEL), 1.0),
        'c_sample': nrm(k[9], (DEC_BATCH, D_MODEL), 1.0),
        'ada_w': nrm(k[10], (DEPTH, D_MODEL, N_MOD * D_MODEL), 0.6 * D_MODEL ** -0.5),
        'ada_b': nrm(k[11], (DEPTH, N_MOD * D_MODEL), 0.02),
        'norm_g': gain(k[12], (DEPTH, 3, D_MODEL)),
        'ffn_w_in': nrm(k[13], (DEPTH, 2, D_MODEL, 2 * D_FF), D_MODEL ** -0.5),
        'ffn_w_out': nrm(k[14], (DEPTH, 2, D_FF, D_MODEL), D_FF ** -0.5),
        'ab_w_in': nrm(k[15], (N_EVEN, D_MODEL, AB_IN), D_MODEL ** -0.5),
        'a_lambda': nrm(k[16], (N_EVEN, 4, A_HEAD_DIM), 0.1),
        'a_subln_g': gain(k[17], (N_EVEN, A_V_DIM)),
        'mla_q_norm_g': gain(k[18], (N_EVEN, MLA_Q_RANK)),
        'mla_w_uq': nrm(k[19], (N_EVEN, MLA_Q_RANK, MLA_HEADS * (MLA_NOPE + MLA_ROPE)), MLA_Q_RANK ** -0.5),
        'mla_kv_norm_g': gain(k[20], (N_EVEN, MLA_KV_RANK)),
        'mla_w_ukv': nrm(k[21], (N_EVEN, MLA_KV_RANK, MLA_HEADS * (MLA_NOPE + MLA_V)), MLA_KV_RANK ** -0.5),
        'ab_w_out': nrm(k[22], (N_EVEN, AB_OUT, D_MODEL), AB_OUT ** -0.5),
        'c_w_in': nrm(k[23], (N_ODD, D_MODEL, 3 * C_HEADS * C_HEAD_DIM), D_MODEL ** -0.5),
        'c_rel_bias': nrm(k[24], (N_ODD, C_HEADS, 2 * REL_CLIP + 1), 0.2),
        'c_w_out': nrm(k[25], (N_ODD, C_HEADS * C_HEAD_DIM, D_MODEL), (C_HEADS * C_HEAD_DIM) ** -0.5),
        'final_norm_g': gain(k[26], (D_MODEL,)),
    }


def reference(x_prompt, x_sample, cache_a_k, cache_a_v, cache_mla_latent, cache_mla_krope,
              cache_c_k, cache_c_v, c_prompt, c_sample, ada_w, ada_b, norm_g, ffn_w_in, ffn_w_out,
              ab_w_in, a_lambda, a_subln_g, mla_q_norm_g, mla_w_uq, mla_kv_norm_g, mla_w_ukv,
              ab_w_out, c_w_in, c_rel_bias, c_w_out, final_norm_g):
    prm = dict(ada_w=ada_w, ada_b=ada_b, norm_g=norm_g, ffn_w_in=ffn_w_in, ffn_w_out=ffn_w_out,
               ab_w_in=ab_w_in, a_lambda=a_lambda, a_subln_g=a_subln_g, mla_q_norm_g=mla_q_norm_g,
               mla_w_uq=mla_w_uq, mla_kv_norm_g=mla_kv_norm_g, mla_w_ukv=mla_w_ukv,
               ab_w_out=ab_w_out, c_w_in=c_w_in, c_rel_bias=c_rel_bias, c_w_out=c_w_out,
               final_norm_g=final_norm_g)
    y_prompt, ab_p, c_p = _trunk(x_prompt, c_prompt, None, prm)
    past = dict(a_k=cache_a_k, a_v=cache_a_v, mla_latent=cache_mla_latent,
                mla_krope=cache_mla_krope, c_k=cache_c_k, c_v=cache_c_v)
    y_sample, ab_s, c_s = _trunk(x_sample, c_sample, past, prm)
    a_k_p, a_v_p, lat_p, kr_p = ab_p
    a_k_s, a_v_s, lat_s, kr_s = ab_s
    c_k_p, c_v_p = c_p
    c_k_s, c_v_s = c_s
    return (y_prompt, y_sample, a_k_p, a_k_s, a_v_p, a_v_s, lat_p, lat_s, kr_p, kr_s,
            c_k_p, c_k_s, c_v_p, c_v_s)
```

```python
import functools
import math

import jax
import jax.numpy as jnp
from jax import lax
from jax.experimental import pallas as pl
from jax.experimental.pallas import tpu as pltpu

F32 = jnp.float32
BF16 = jnp.bfloat16

EPS = 1e-5
CHUNK = 64
CHUNK_SHIFT = CHUNK.bit_length() - 1
ROPE_THETA = 10000.0
N_MOD = 9
A_HEADS = 4
A_HEAD_DIM = 64
A_V_DIM = 2 * A_HEAD_DIM
A_SCALE = A_HEAD_DIM ** -0.5
MLA_HEADS = 4
MLA_NOPE = 128
MLA_ROPE = 64
MLA_V = 128
MLA_SCALE = (MLA_NOPE + MLA_ROPE) ** -0.5
C_HEADS = 16
C_HEAD_DIM = 64
C_SCALE = C_HEAD_DIM ** -0.5
C_PAST_CHUNKS = 8
C_BAND_PAST = C_PAST_CHUNKS * CHUNK
REL_CLIP = 128

LANES = 128
VMEM_LIMIT_BYTES = 56 * 1024 * 1024

ROW_TILE = 512
Q_TILE_AB = 256
Q_TILE_C = 128
FF_CHUNK = 256


def _dot(a, b):
    return jnp.dot(a, b, preferred_element_type=F32)


def _dot_t(a, b):
    return lax.dot_general(a, b, (((1,), (1,)), ((), ())), preferred_element_type=F32)


def _rms_scale(x):
    return x * lax.rsqrt(jnp.mean(x * x, axis=-1, keepdims=True) + EPS)


def _modulated(x3, g, shift, scale):
    return _rms_scale(x3) * g * (1.0 + scale) + shift


def _lane_lo64(shape):
    return (lax.broadcasted_iota(jnp.int32, shape, len(shape) - 1) & (LANES - 1)) < CHUNK


def _chunk_of(pos):
    return jnp.right_shift(pos, CHUNK_SHIFT)


def _params(n_grid):
    return pltpu.CompilerParams(dimension_semantics=("arbitrary",) * n_grid,
                                vmem_limit_bytes=VMEM_LIMIT_BYTES)


def _vmem_spec():
    return pl.BlockSpec(memory_space=pltpu.VMEM)


def _mod_kernel(c_ref, w_ref, b_ref, o_ref):
    c = c_ref[...]
    sc = (c * jax.nn.sigmoid(c)).astype(BF16)
    o_ref[0] = _dot(sc, w_ref[0].astype(BF16)) + b_ref[0]


def _modulation(c_all, ada_w, ada_b):
    depth, d, n = ada_w.shape
    rows = c_all.shape[0]
    tn = 1536
    return pl.pallas_call(
        _mod_kernel,
        grid=(depth, n // tn),
        in_specs=[pl.BlockSpec((rows, d), lambda l, j: (0, 0)),
                  pl.BlockSpec((1, d, tn), lambda l, j: (l, 0, j)),
                  pl.BlockSpec((1, 1, tn), lambda l, j: (l, 0, j))],
        out_specs=pl.BlockSpec((1, rows, tn), lambda l, j: (l, 0, j)),
        out_shape=jax.ShapeDtypeStruct((depth, rows, n), F32),
        compiler_params=_params(2),
        name="adaln_mod",
    )(c_all, ada_w, ada_b.reshape(depth, 1, n))


def _ffn_kernel(*refs, k0, g_row, final):
    if final:
        x_ref, mod_ref, g_ref, win_ref, wout_ref, gf_ref, o_ref, act_ref = refs
    else:
        x_ref, mod_ref, g_ref, win_ref, wout_ref, o_ref, act_ref = refs
    nb, s, d = x_ref.shape
    ff = wout_ref.shape[0]
    x = x_ref[...]
    h = _modulated(x, g_ref[g_row:g_row + 1, :], mod_ref[:, k0], mod_ref[:, k0 + 1])
    h = h.reshape(nb * s, d).astype(BF16)
    for c in range(ff // FF_CHUNK):
        lo = c * FF_CHUNK
        gate = _dot(h, win_ref[:, lo:lo + FF_CHUNK])
        up = _dot(h, win_ref[:, ff + lo:ff + lo + FF_CHUNK])
        act_ref[:, lo:lo + FF_CHUNK] = (gate * jax.nn.sigmoid(gate) * up).astype(BF16)
    y = _dot(act_ref[...], wout_ref[...]).reshape(nb, s, d)
    out = x + 0.5 * mod_ref[:, k0 + 2] * y
    if final:
        out = _rms_scale(out) * gf_ref[...]
    o_ref[...] = out


def _ffn(x, mod, g, win, wout, *, k0, g_row, nb, s, final_g=None):
    bsz, length, d = x.shape
    ff = wout.shape[0]
    final = final_g is not None
    in_specs = [pl.BlockSpec((nb, s, d), lambda i, j: (i, j, 0)),
                pl.BlockSpec((nb, N_MOD, 1, d), lambda i, j: (i, 0, 0, 0)),
                _vmem_spec(), _vmem_spec(), _vmem_spec()]
    args = [x, mod, g, win, wout]
    if final:
        in_specs.append(_vmem_spec())
        args.append(final_g.reshape(1, d))
    return pl.pallas_call(
        functools.partial(_ffn_kernel, k0=k0, g_row=g_row, final=final),
        grid=(bsz // nb, length // s),
        in_specs=in_specs,
        out_specs=pl.BlockSpec((nb, s, d), lambda i, j: (i, j, 0)),
        out_shape=jax.ShapeDtypeStruct(x.shape, F32),
        scratch_shapes=[pltpu.VMEM((nb * s, ff), BF16)],
        compiler_params=_params(2),
        name="ffn",
    )(*args)


def _mla_kv(lat_b, kr_pad, wkn_ref, wv_ref, km_ref, vm_ref):
    nb, s, _ = km_ref.shape
    kn = _dot(lat_b, wkn_ref[...])
    vm_ref[...] = _dot(lat_b, wv_ref[...]).astype(BF16).reshape(vm_ref.shape)
    blk = MLA_NOPE + LANES
    kr_b = kr_pad.astype(BF16).reshape(nb, s, LANES)
    for h in range(MLA_HEADS):
        km_ref[:, :, h * blk:h * blk + MLA_NOPE] = (
            kn[:, h * MLA_NOPE:(h + 1) * MLA_NOPE].astype(BF16).reshape(nb, s, MLA_NOPE))
        km_ref[:, :, h * blk + MLA_NOPE:(h + 1) * blk] = kr_b


def _ab_proj_kernel(x_ref, mod_ref, g_ref, cos_ref, sin_ref, win_ref, qg_ref, wuq_ref, kvg_ref,
                    wkn_ref, wv_ref,
                    aq_ref, akb_ref, avb_ref, qm_ref, km_ref, vm_ref, ak_ref, av_ref, lat_ref, kr_ref):
    nb, s, d = x_ref.shape
    rows = nb * s
    h = _modulated(x_ref[...], g_ref[1:2, :], mod_ref[:, 3], mod_ref[:, 4])
    h = h.reshape(rows, d).astype(BF16)
    cos = cos_ref[...]
    sin = sin_ref[...]
    lo32 = (lax.broadcasted_iota(jnp.int32, (rows, LANES), 1) & (CHUNK - 1)) < (CHUNK // 2)

    def rope(t):
        sw = jnp.where(lo32, pltpu.roll(t, LANES - CHUNK // 2, 1), pltpu.roll(t, CHUNK // 2, 1))
        return t.reshape(nb, s, LANES) * cos + sw.reshape(nb, s, LANES) * sin

    a_qk = A_HEADS * 2 * A_HEAD_DIM
    a_v = A_HEADS * A_V_DIM
    q_rank = qg_ref.shape[1]
    kv_rank = kvg_ref.shape[1]
    c0 = 0
    aq = _dot(h, win_ref[:, c0:c0 + a_qk])
    c0 += a_qk
    ak = _dot(h, win_ref[:, c0:c0 + a_qk])
    c0 += a_qk
    for b in range(a_qk // LANES):
        sl = slice(b * LANES, (b + 1) * LANES)
        aq_ref[:, :, sl] = (rope(aq[:, sl]) * A_SCALE).astype(BF16)
        kb = rope(ak[:, sl])
        ak_ref[:, :, sl] = kb
        akb_ref[:, :, sl] = kb.astype(BF16)
    av = _dot(h, win_ref[:, c0:c0 + a_v]).reshape(nb, s, a_v)
    c0 += a_v
    av_ref[...] = av
    avb_ref[...] = av.astype(BF16)
    cq = _dot(h, win_ref[:, c0:c0 + q_rank])
    c0 += q_rank
    cqn = (_rms_scale(cq) * qg_ref[...]).astype(BF16)
    bq = _dot(cqn, wuq_ref[...])
    blk = MLA_NOPE + LANES
    for hh in range(MLA_HEADS):
        nope = bq[:, hh * MLA_NOPE:(hh + 1) * MLA_NOPE] * MLA_SCALE
        qm_ref[:, :, hh * blk:hh * blk + MLA_NOPE] = nope.astype(BF16).reshape(nb, s, MLA_NOPE)
        off = MLA_HEADS * MLA_NOPE + hh * LANES
        qm_ref[:, :, hh * blk + MLA_NOPE:(hh + 1) * blk] = (rope(bq[:, off:off + LANES]) * MLA_SCALE).astype(BF16)
    ckv = _dot(h, win_ref[:, c0:c0 + kv_rank])
    c0 += kv_rank
    lat = _rms_scale(ckv) * kvg_ref[...]
    lat_ref[...] = lat.reshape(nb, s, kv_rank)
    kr_pad = rope(_dot(h, win_ref[:, c0:c0 + LANES]))
    kr_ref[...] = kr_pad[:, :, :MLA_ROPE]
    _mla_kv(lat.astype(BF16), kr_pad.reshape(rows, LANES), wkn_ref, wv_ref, km_ref, vm_ref)


def _ab_proj(x, mod, g, cos, sin, w, *, nb, s):
    bsz, length, d = x.shape
    a_qk = A_HEADS * 2 * A_HEAD_DIM
    a_v = A_HEADS * A_V_DIM
    kv_rank = w["kvg"].shape[1]
    m_w = MLA_HEADS * (MLA_NOPE + LANES)

    def rowspec(n):
        return pl.BlockSpec((nb, s, n), lambda i, j: (i, j, 0))

    outs = [(a_qk, BF16), (a_qk, BF16), (a_v, BF16), (m_w, BF16), (m_w, BF16), (MLA_HEADS * MLA_V, BF16),
            (a_qk, F32), (a_v, F32), (kv_rank, F32), (MLA_ROPE, F32)]
    return pl.pallas_call(
        _ab_proj_kernel,
        grid=(bsz // nb, length // s),
        in_specs=[rowspec(d),
                  pl.BlockSpec((nb, N_MOD, 1, d), lambda i, j: (i, 0, 0, 0)),
                  _vmem_spec(),
                  pl.BlockSpec((s, LANES), lambda i, j: (j, 0)),
                  pl.BlockSpec((s, LANES), lambda i, j: (j, 0)),
                  _vmem_spec(), _vmem_spec(), _vmem_spec(), _vmem_spec(), _vmem_spec(), _vmem_spec()],
        out_specs=[rowspec(n) for n, _ in outs],
        out_shape=[jax.ShapeDtypeStruct((bsz, length, n), dt) for n, dt in outs],
        compiler_params=_params(2),
        name="ab_proj",
    )(x, mod, g, cos, sin, w["win"], w["qg"], w["wuq"], w["kvg"], w["wkn"], w["wv"])


def _mla_kv_kernel(lat_ref, kr_ref, wkn_ref, wv_ref, km_ref, vm_ref):
    _mla_kv(lat_ref[0].astype(BF16), kr_ref[0], wkn_ref, wv_ref, km_ref, vm_ref)


def _mla_kv_cache(lat_all, kr_all, w):
    bsz, lk, r = lat_all.shape
    m_w = MLA_HEADS * (MLA_NOPE + LANES)
    return pl.pallas_call(
        _mla_kv_kernel,
        grid=(bsz,),
        in_specs=[pl.BlockSpec((1, lk, r), lambda i: (i, 0, 0)),
                  pl.BlockSpec((1, lk, LANES), lambda i: (i, 0, 0)),
                  _vmem_spec(), _vmem_spec()],
        out_specs=[pl.BlockSpec((1, lk, m_w), lambda i: (i, 0, 0)),
                   pl.BlockSpec((1, lk, MLA_HEADS * MLA_V), lambda i: (i, 0, 0))],
        out_shape=[jax.ShapeDtypeStruct((bsz, lk, m_w), BF16),
                   jax.ShapeDtypeStruct((bsz, lk, MLA_HEADS * MLA_V), BF16)],
        compiler_params=_params(1),
        name="mla_kv_cache",
    )(lat_all, kr_all, w["wkn"], w["wv"])


def _softmax_parts(s, visible):
    s = jnp.where(visible, s, -jnp.inf)
    e = jnp.exp(s - jnp.max(s, axis=-1, keepdims=True))
    return e, 1.0 / jnp.sum(e, axis=-1, keepdims=True)


def _ab_attn_kernel(x_ref, mod_ref, aq_ref, ak_ref, av_ref, qm_ref, km_ref, vm_ref, lam_ref, sg_ref,
                    wo_ref, o_ref, cat_ref, *, q_pos0, lk_valid, lam_init):
    _, tq, d = x_ref.shape
    lk = ak_ref.shape[1]
    q_pos = q_pos0 + pl.program_id(1) * tq + lax.broadcasted_iota(jnp.int32, (tq, lk), 0)
    k_idx = lax.broadcasted_iota(jnp.int32, (tq, lk), 1)
    visible = (_chunk_of(k_idx) <= _chunk_of(q_pos)) & (k_idx < lk_valid)
    lf = lam_ref[...]
    lam = (jnp.exp(jnp.sum(lf[0:1] * lf[1:2], axis=-1, keepdims=True))
           - jnp.exp(jnp.sum(lf[2:3] * lf[3:4], axis=-1, keepdims=True)) + lam_init)
    lo64 = _lane_lo64((tq, A_V_DIM))
    zero = jnp.zeros((tq, A_V_DIM), BF16)
    blk = MLA_NOPE + LANES
    for h in range(A_HEADS):
        sl = slice(h * A_V_DIM, (h + 1) * A_V_DIM)
        qp = aq_ref[0, :, sl]
        kp = ak_ref[0, :, sl]
        e0, r0 = _softmax_parts(_dot_t(jnp.where(lo64, qp, zero), kp), visible)
        e1, r1 = _softmax_parts(_dot_t(jnp.where(lo64, zero, qp), kp), visible)
        w = (e0 * r0 - e1 * (lam * r1)).astype(BF16)
        oa = _dot(w, av_ref[0, :, sl])
        oa = _rms_scale(oa) * sg_ref[...] * (1.0 - lam_init)
        cat_ref[:, sl] = oa.astype(BF16)
        msl = slice(h * blk, (h + 1) * blk)
        e, r = _softmax_parts(_dot_t(qm_ref[0, :, msl], km_ref[0, :, msl]), visible)
        ob = _dot((e * r).astype(BF16), vm_ref[0, :, h * MLA_V:(h + 1) * MLA_V])
        off = A_HEADS * A_V_DIM + h * MLA_V
        cat_ref[:, off:off + MLA_V] = ob.astype(BF16)
    mix = _dot(cat_ref[...], wo_ref[...])
    o_ref[0] = x_ref[0] + mod_ref[0, 5] * mix


def _ab_attn(x, mod, aq, ak, av, qm, km, vm, lam_p, sg, wo, *, tq, q_pos0, lk_valid, lam_init):
    bsz, lq, d = x.shape
    lk = ak.shape[1]

    def qspec(n):
        return pl.BlockSpec((1, tq, n), lambda i, j: (i, j, 0))

    def kspec(n):
        return pl.BlockSpec((1, lk, n), lambda i, j: (i, 0, 0))

    return pl.pallas_call(
        functools.partial(_ab_attn_kernel, q_pos0=q_pos0, lk_valid=lk_valid, lam_init=lam_init),
        grid=(bsz, lq // tq),
        in_specs=[qspec(d),
                  pl.BlockSpec((1, N_MOD, 1, d), lambda i, j: (i, 0, 0, 0)),
                  qspec(aq.shape[2]), kspec(ak.shape[2]), kspec(av.shape[2]),
                  qspec(qm.shape[2]), kspec(km.shape[2]), kspec(vm.shape[2]),
                  _vmem_spec(), _vmem_spec(), _vmem_spec()],
        out_specs=qspec(d),
        out_shape=jax.ShapeDtypeStruct(x.shape, F32),
        scratch_shapes=[pltpu.VMEM((tq, wo.shape[0]), BF16)],
        compiler_params=_params(2),
        name="ab_attn",
    )(x, mod, aq, ak, av, qm, km, vm, lam_p, sg, wo)


def _c_proj_kernel(x_ref, mod_ref, g_ref, win_ref, q_ref, kp_ref, vp_ref, ck_ref, cv_ref):
    j = pl.program_id(1)
    hd = C_HEADS * C_HEAD_DIM

    @pl.when(j == 0)
    def _():
        kp_ref[...] = jnp.zeros(kp_ref.shape, BF16)
        vp_ref[...] = jnp.zeros(vp_ref.shape, BF16)

    @pl.when(j > 0)
    def _():
        _, s, d = x_ref.shape
        h = _modulated(x_ref[...], g_ref[1:2, :], mod_ref[:, 3], mod_ref[:, 4])
        h = h.reshape(s, d).astype(BF16)
        q_ref[0] = (_dot(h, win_ref[:, :hd]) * C_SCALE).astype(BF16)
        k = _dot(h, win_ref[:, hd:2 * hd])
        v = _dot(h, win_ref[:, 2 * hd:])
        kp_ref[0] = k.astype(BF16)
        vp_ref[0] = v.astype(BF16)

        @pl.when(j == pl.num_programs(1) - 1)
        def _():
            ck_ref[0] = k
            cv_ref[0] = v


def _c_proj(x, mod, g, win):
    bsz, length, d = x.shape
    hd = C_HEADS * C_HEAD_DIM
    s = C_BAND_PAST
    nj = length // s

    def prev(i, j):
        return (i, jnp.maximum(j - 1, 0), 0)

    return pl.pallas_call(
        _c_proj_kernel,
        grid=(bsz, nj + 1),
        in_specs=[pl.BlockSpec((1, s, d), prev),
                  pl.BlockSpec((1, N_MOD, 1, d), lambda i, j: (i, 0, 0, 0)),
                  _vmem_spec(), _vmem_spec()],
        out_specs=[pl.BlockSpec((1, s, hd), prev),
                   pl.BlockSpec((1, s, hd), lambda i, j: (i, j, 0)),
                   pl.BlockSpec((1, s, hd), lambda i, j: (i, j, 0)),
                   pl.BlockSpec((1, s, hd), lambda i, j: (i, 0, 0)),
                   pl.BlockSpec((1, s, hd), lambda i, j: (i, 0, 0))],
        out_shape=[jax.ShapeDtypeStruct((bsz, length, hd), BF16),
                   jax.ShapeDtypeStruct((bsz, length + s, hd), BF16),
                   jax.ShapeDtypeStruct((bsz, length + s, hd), BF16),
                   jax.ShapeDtypeStruct((bsz, s, hd), F32),
                   jax.ShapeDtypeStruct((bsz, s, hd), F32)],
        compiler_params=_params(2),
        name="c_proj",
    )(x, mod, g, win)


def _modproj_kernel(x_ref, mod_ref, g_ref, win_ref, o_ref):
    nb, s, d = x_ref.shape
    h = _modulated(x_ref[...], g_ref[1:2, :], mod_ref[:, 3], mod_ref[:, 4])
    h = h.reshape(nb * s, d).astype(BF16)
    o_ref[...] = _dot(h, win_ref[...]).reshape(o_ref.shape)


def _modproj(x, mod, g, win):
    bsz, s, d = x.shape
    n = win.shape[1]
    return pl.pallas_call(
        _modproj_kernel,
        grid=(1,),
        in_specs=[pl.BlockSpec((bsz, s, d), lambda i: (0, 0, 0)),
                  pl.BlockSpec((bsz, N_MOD, 1, d), lambda i: (0, 0, 0, 0)),
                  _vmem_spec(), _vmem_spec()],
        out_specs=pl.BlockSpec((bsz, s, n), lambda i: (0, 0, 0)),
        out_shape=jax.ShapeDtypeStruct((bsz, s, n), F32),
        compiler_params=_params(1),
        name="c_proj_sample",
    )(x, mod, g, win)


def _c_attn_kernel(x_ref, mod_ref, q_ref, k_ref, v_ref, tb_ref, wo_ref, o_ref, cat_ref, *,
                   window, pos_base, lk_valid):
    _, tq, d = x_ref.shape
    q0 = pl.multiple_of(pl.program_id(1) * tq, tq)
    kw = k_ref[0, pl.ds(q0, window), :]
    vw = v_ref[0, pl.ds(q0, window), :]
    c_idx = lax.broadcasted_iota(jnp.int32, (tq, window), 1)
    visible = ((pos_base + q0 + c_idx) >= 0) & (c_idx < lk_valid)
    lo64 = _lane_lo64((tq, LANES))
    zero = jnp.zeros((tq, LANES), BF16)
    for p in range(C_HEADS // 2):
        sl = slice(p * LANES, (p + 1) * LANES)
        qp = q_ref[0, :, sl]
        kp = kw[:, sl]
        vp = vw[:, sl]
        halves = []
        for t in range(2):
            qt = jnp.where(lo64, qp, zero) if t == 0 else jnp.where(lo64, zero, qp)
            s = _dot_t(qt, kp) + tb_ref[2 * p + t]
            e, r = _softmax_parts(s, visible)
            halves.append(_dot((e * r).astype(BF16), vp))
        cat_ref[:, sl] = jnp.where(lo64, halves[0], halves[1]).astype(BF16)
    mix = _dot(cat_ref[...], wo_ref[...])
    o_ref[0] = x_ref[0] + mod_ref[0, 5] * mix


def _c_attn(x, mod, q, kpad, vpad, tb, wo, *, tq, window, pos_base, lk_valid):
    bsz, lq, d = x.shape
    lkp = kpad.shape[1]
    hd = q.shape[2]

    def qspec(n):
        return pl.BlockSpec((1, tq, n), lambda i, j: (i, j, 0))

    def kspec(n):
        return pl.BlockSpec((1, lkp, n), lambda i, j: (i, 0, 0))

    return pl.pallas_call(
        functools.partial(_c_attn_kernel, window=window, pos_base=pos_base, lk_valid=lk_valid),
        grid=(bsz, lq // tq),
        in_specs=[qspec(d),
                  pl.BlockSpec((1, N_MOD, 1, d), lambda i, j: (i, 0, 0, 0)),
                  qspec(hd), kspec(hd), kspec(hd), _vmem_spec(), _vmem_spec()],
        out_specs=qspec(d),
        out_shape=jax.ShapeDtypeStruct(x.shape, F32),
        scratch_shapes=[pltpu.VMEM((tq, hd), BF16)],
        compiler_params=_params(2),
        name="c_attn",
    )(x, mod, q, kpad, vpad, tb, wo)


def _rope_tables(pos0, length):
    half = CHUNK // 2
    inv = 1.0 / (ROPE_THETA ** (jnp.arange(half, dtype=F32) * (2.0 / CHUNK)))
    ang = (pos0 + jnp.arange(length, dtype=jnp.int32)).astype(F32)[:, None] * inv[None, :]
    cos = jnp.cos(ang)
    sin = jnp.sin(ang)
    cos = jnp.concatenate([cos, cos, cos, cos], axis=-1)
    sin = jnp.concatenate([-sin, sin, -sin, sin], axis=-1)
    return cos, sin


def _band_bias_table(rel_bias, tq, window):
    a = jnp.arange(tq, dtype=jnp.int32)[:, None]
    c = jnp.arange(window, dtype=jnp.int32)[None, :]
    rel = jnp.clip(c - C_BAND_PAST - a, -REL_CLIP, REL_CLIP) + REL_CLIP
    in_band = (c // CHUNK >= a // CHUNK) & (c // CHUNK - C_PAST_CHUNKS <= a // CHUNK)
    return jnp.where(in_band[None], rel_bias[:, rel].astype(F32), -jnp.inf)


def _pad_rows(a, rows):
    return jnp.pad(a, ((0, 0), (0, rows - a.shape[1]), (0, 0)))


def _round_up(n, m):
    return (n + m - 1) // m * m


def _prep_weights(ffn_w_in, ffn_w_out, ab_w_in, mla_q_norm_g, mla_w_uq, mla_kv_norm_g, mla_w_ukv,
                  ab_w_out, c_w_in, c_w_out):
    depth = ffn_w_in.shape[0]
    w = {"ffn_in": ffn_w_in.astype(BF16), "ffn_out": ffn_w_out.astype(BF16), "ab": [], "c": []}
    for i in range(ab_w_in.shape[0]):
        win = ab_w_in[i]
        pad = _round_up(win.shape[1], LANES) - win.shape[1]
        win = jnp.pad(win, ((0, 0), (0, pad))).astype(BF16)
        q_rank = mla_w_uq.shape[1]
        uq = mla_w_uq[i].reshape(q_rank, MLA_HEADS, MLA_NOPE + MLA_ROPE)
        uq_rope = jnp.pad(uq[:, :, MLA_NOPE:], ((0, 0), (0, 0), (0, LANES - MLA_ROPE)))
        wuq = jnp.concatenate([uq[:, :, :MLA_NOPE].reshape(q_rank, -1), uq_rope.reshape(q_rank, -1)], axis=1)
        kv_rank = mla_w_ukv.shape[1]
        ukv = mla_w_ukv[i].reshape(kv_rank, MLA_HEADS, MLA_NOPE + MLA_V)
        w["ab"].append({
            "win": win,
            "qg": mla_q_norm_g[i].reshape(1, -1),
            "wuq": wuq.astype(BF16),
            "kvg": mla_kv_norm_g[i].reshape(1, -1),
            "wkn": ukv[:, :, :MLA_NOPE].reshape(kv_rank, -1).astype(BF16),
            "wv": ukv[:, :, MLA_NOPE:].reshape(kv_rank, -1).astype(BF16),
            "wo": ab_w_out[i].astype(BF16),
        })
    for i in range(c_w_in.shape[0]):
        w["c"].append({"win": c_w_in[i].astype(BF16), "wo": c_w_out[i].astype(BF16)})
    del depth
    return w


def _trunk(x, mod, past, prm, w, *, prompt):
    bsz, length, d = x.shape
    depth = mod.shape[0]
    if prompt:
        nb, s = 1, min(ROW_TILE, length)
        pos0 = 0
    else:
        nb, s = bsz, length
        pos0 = past["a_k"].shape[2]
    cos, sin = _rope_tables(pos0, length)
    ab_rows, c_rows = [], []
    for l in range(depth):
        g = prm["norm_g"][l]
        x = _ffn(x, mod[l], g, w["ffn_in"][l, 0], w["ffn_out"][l, 0], k0=0, g_row=0, nb=nb, s=s)
        i = l // 2
        if l % 2 == 0:
            wl = w["ab"][i]
            lam_init = 0.8 - 0.6 * math.exp(-0.3 * l)
            aq, akb, avb, qm, km, vm, ak, av, lat, kr = _ab_proj(x, mod[l], g, cos, sin, wl, nb=nb, s=s)
            if prompt:
                tq, lk_valid = min(Q_TILE_AB, length), length
            else:
                lk_valid = pos0 + length
                lk = _round_up(lk_valid, LANES)
                pa_k = past["a_k"][i].reshape(bsz, pos0, -1)
                pa_v = past["a_v"][i].reshape(bsz, pos0, -1)
                akb = _pad_rows(jnp.concatenate([pa_k.astype(BF16), akb], axis=1), lk)
                avb = _pad_rows(jnp.concatenate([pa_v.astype(BF16), avb], axis=1), lk)
                lat_all = _pad_rows(jnp.concatenate([past["mla_latent"][i], lat], axis=1), lk)
                kr_all = jnp.concatenate([past["mla_krope"][i], kr], axis=1)
                kr_all = jnp.pad(kr_all, ((0, 0), (0, lk - lk_valid), (0, LANES - MLA_ROPE)))
                km, vm = _mla_kv_cache(lat_all, kr_all, wl)
                tq = length
            x = _ab_attn(x, mod[l], aq, akb, avb, qm, km, vm, prm["a_lambda"][i],
                         prm["a_subln_g"][i].reshape(1, -1), wl["wo"],
                         tq=tq, q_pos0=pos0, lk_valid=lk_valid, lam_init=lam_init)
            ab_rows.append((ak.reshape(bsz, length, A_HEADS, 2, A_HEAD_DIM),
                            av.reshape(bsz, length, A_HEADS, A_V_DIM), lat, kr))
        else:
            wl = w["c"][i]
            hd = C_HEADS * C_HEAD_DIM
            if prompt:
                tq = min(Q_TILE_C, length)
                window = C_BAND_PAST + tq
                q, kpad, vpad, ck, cv = _c_proj(x, mod[l], g, wl["win"])
                pos_base, lk_valid = -C_BAND_PAST, window
            else:
                pk = past["c_k"][i].reshape(bsz, -1, hd)
                pv = past["c_v"][i].reshape(bsz, -1, hd)
                lc = pk.shape[1]
                qkv = _modproj(x, mod[l], g, wl["win"])
                q = (qkv[:, :, :hd] * C_SCALE).astype(BF16)
                k_all = jnp.concatenate([pk, qkv[:, :, hd:2 * hd]], axis=1)
                v_all = jnp.concatenate([pv, qkv[:, :, 2 * hd:]], axis=1)
                tq = length
                window = _round_up(C_BAND_PAST + tq, LANES)
                kpad = _pad_rows(k_all.astype(BF16), window)
                vpad = _pad_rows(v_all.astype(BF16), window)
                ck, cv = k_all[:, -lc:], v_all[:, -lc:]
                pos_base, lk_valid = pos0 - lc, lc + length
            tb = _band_bias_table(prm["c_rel_bias"][i], tq, window)
            x = _c_attn(x, mod[l], q, kpad, vpad, tb, wl["wo"], tq=tq, window=window,
                        pos_base=pos_base, lk_valid=lk_valid)
            c_rows.append((ck.reshape(bsz, -1, C_HEADS, C_HEAD_DIM), cv.reshape(bsz, -1, C_HEADS, C_HEAD_DIM)))
        x = _ffn(x, mod[l], g, w["ffn_in"][l, 1], w["ffn_out"][l, 1], k0=6, g_row=2, nb=nb, s=s,
                 final_g=prm["final_norm_g"] if l == depth - 1 else None)
    ab_state = tuple(jnp.stack([r[j] for r in ab_rows]) for j in range(4))
    c_state = tuple(jnp.stack([r[j] for r in c_rows]) for j in range(2))
    return x, ab_state, c_state


def kernel(x_prompt, x_sample, cache_a_k, cache_a_v, cache_mla_latent, cache_mla_krope, cache_c_k, cache_c_v,
           c_prompt, c_sample, ada_w, ada_b, norm_g, ffn_w_in, ffn_w_out, ab_w_in, a_lambda, a_subln_g,
           mla_q_norm_g, mla_w_uq, mla_kv_norm_g, mla_w_ukv, ab_w_out, c_w_in, c_rel_bias, c_w_out,
           final_norm_g):
    bp = x_prompt.shape[0]
    d = x_prompt.shape[2]
    depth = ada_w.shape[0]
    c_all = jnp.concatenate([c_prompt, c_sample], axis=0)
    n_seq = c_all.shape[0]
    c_all = jnp.pad(c_all, ((0, _round_up(n_seq, 16) - n_seq), (0, 0)))
    mod = _modulation(c_all, ada_w, ada_b)[:, :n_seq]
    mod = mod.reshape(depth, n_seq, N_MOD, 1, d)
    w = _prep_weights(ffn_w_in, ffn_w_out, ab_w_in, mla_q_norm_g, mla_w_uq, mla_kv_norm_g, mla_w_ukv,
                      ab_w_out, c_w_in, c_w_out)
    prm = dict(norm_g=norm_g, a_lambda=a_lambda, a_subln_g=a_subln_g, c_rel_bias=c_rel_bias,
               final_norm_g=final_norm_g)
    y_p, ab_p, c_p = _trunk(x_prompt, mod[:, :bp], None, prm, w, prompt=True)
    past = dict(a_k=cache_a_k, a_v=cache_a_v, mla_latent=cache_mla_latent, mla_krope=cache_mla_krope,
                c_k=cache_c_k, c_v=cache_c_v)
    y_s, ab_s, c_s = _trunk(x_sample, mod[:, bp:], past, prm, w, prompt=False)
    return (y_p, y_s, ab_p[0], ab_s[0], ab_p[1], ab_s[1], ab_p[2], ab_s[2], ab_p[3], ab_s[3],
            c_p[0], c_s[0], c_p[1], c_s[1])
```

```python
import functools
import math

import jax
import jax.numpy as jnp
from jax import lax
from jax.experimental import pallas as pl
from jax.experimental.pallas import tpu as pltpu

F32 = jnp.float32
BF16 = jnp.bfloat16

EPS = 1e-5
CHUNK = 64
CHUNK_SHIFT = CHUNK.bit_length() - 1
ROPE_THETA = 10000.0
N_MOD = 9
LOG2E = math.log2(math.e)
A_HEADS = 4
A_HEAD_DIM = 64
A_V_DIM = 2 * A_HEAD_DIM
A_SCALE = A_HEAD_DIM ** -0.5 * LOG2E
MLA_HEADS = 4
MLA_NOPE = 128
MLA_ROPE = 64
MLA_V = 128
MLA_SCALE = (MLA_NOPE + MLA_ROPE) ** -0.5 * LOG2E
C_HEADS = 16
C_HEAD_DIM = 64
C_SCALE = C_HEAD_DIM ** -0.5 * LOG2E
C_PAST_CHUNKS = 8
C_BAND_PAST = C_PAST_CHUNKS * CHUNK
REL_CLIP = 128

LANES = 128
VMEM_LIMIT_BYTES = 56 * 1024 * 1024

ROW_TILE = 512
Q_TILE_AB = 256
K_TILE_AB = 256
Q_TILE_C = 256
FF_CHUNK = 256
MLA_BLK = MLA_NOPE + LANES


def _dot(a, b):
    return jnp.dot(a, b, preferred_element_type=F32)


def _dot_t(a, b):
    return lax.dot_general(a, b, (((1,), (1,)), ((), ())), preferred_element_type=F32)


def _rms_scale(x):
    return x * lax.rsqrt(jnp.mean(x * x, axis=-1, keepdims=True) + EPS)


def _modulated(x3, g, shift, scale):
    return _rms_scale(x3) * g * (1.0 + scale) + shift


def _lane_lo64(shape):
    return (lax.broadcasted_iota(jnp.int32, shape, len(shape) - 1) & (LANES - 1)) < CHUNK


def _chunk_of(pos):
    return jnp.right_shift(pos, CHUNK_SHIFT)


def _diff_lambda(lam_ref, lam_init):
    lf = lam_ref[...]
    return (jnp.exp(jnp.sum(lf[0:1] * lf[1:2], axis=-1, keepdims=True))
            - jnp.exp(jnp.sum(lf[2:3] * lf[3:4], axis=-1, keepdims=True)) + lam_init)


def _params(n_grid):
    return pltpu.CompilerParams(dimension_semantics=("arbitrary",) * n_grid,
                                vmem_limit_bytes=VMEM_LIMIT_BYTES)


def _vmem_spec():
    return pl.BlockSpec(memory_space=pltpu.VMEM)


def _mod_spec(nb, d):
    return pl.BlockSpec((nb, N_MOD, 1, d), lambda i, j: (i, 0, 0, 0))


def _mod_kernel(c_ref, w_ref, b_ref, o_ref):
    c = c_ref[...]
    sc = (c * jax.nn.sigmoid(c)).astype(BF16)
    o_ref[0] = _dot(sc, w_ref[0].astype(BF16)) + b_ref[0]


def _modulation(c_all, ada_w, ada_b):
    depth, d, n = ada_w.shape
    rows = c_all.shape[0]
    tn = 1536
    return pl.pallas_call(
        _mod_kernel,
        grid=(depth, n // tn),
        in_specs=[pl.BlockSpec((rows, d), lambda l, j: (0, 0)),
                  pl.BlockSpec((1, d, tn), lambda l, j: (l, 0, j)),
                  pl.BlockSpec((1, 1, tn), lambda l, j: (l, 0, j))],
        out_specs=pl.BlockSpec((1, rows, tn), lambda l, j: (l, 0, j)),
        out_shape=jax.ShapeDtypeStruct((depth, rows, n), F32),
        compiler_params=_params(2),
        name="adaln_mod",
    )(c_all, ada_w, ada_b.reshape(depth, 1, n))


def _ffn_kernel(*refs, k0, g_row, final):
    if final:
        x_ref, mod_ref, g_ref, win_ref, wout_ref, gf_ref, o_ref, act_ref = refs
    else:
        x_ref, mod_ref, g_ref, win_ref, wout_ref, o_ref, act_ref = refs
    nb, s, d = x_ref.shape
    ff = wout_ref.shape[0]
    x = x_ref[...]
    h = _modulated(x, g_ref[g_row:g_row + 1, :], mod_ref[:, k0], mod_ref[:, k0 + 1])
    h = h.reshape(nb * s, d).astype(BF16)
    for c in range(ff // FF_CHUNK):
        lo = c * FF_CHUNK
        gate = _dot(h, win_ref[:, lo:lo + FF_CHUNK])
        up = _dot(h, win_ref[:, ff + lo:ff + lo + FF_CHUNK])
        act_ref[:, lo:lo + FF_CHUNK] = (gate * jax.nn.sigmoid(gate) * up).astype(BF16)
    y = _dot(act_ref[...], wout_ref[...]).reshape(nb, s, d)
    out = x + 0.5 * mod_ref[:, k0 + 2] * y
    if final:
        out = _rms_scale(out) * gf_ref[...]
    o_ref[...] = out


def _ffn(x, mod, g, win, wout, *, k0, g_row, nb, s, final_g=None):
    bsz, length, d = x.shape
    ff = wout.shape[0]
    final = final_g is not None
    in_specs = [pl.BlockSpec((nb, s, d), lambda i, j: (i, j, 0)), _mod_spec(nb, d),
                _vmem_spec(), _vmem_spec(), _vmem_spec()]
    args = [x, mod, g, win, wout]
    if final:
        in_specs.append(_vmem_spec())
        args.append(final_g.reshape(1, d))
    return pl.pallas_call(
        functools.partial(_ffn_kernel, k0=k0, g_row=g_row, final=final),
        grid=(bsz // nb, length // s),
        in_specs=in_specs,
        out_specs=pl.BlockSpec((nb, s, d), lambda i, j: (i, j, 0)),
        out_shape=jax.ShapeDtypeStruct(x.shape, F32),
        scratch_shapes=[pltpu.VMEM((nb * s, ff), BF16)],
        compiler_params=_params(2),
        name="ffn",
    )(*args)


def _store_transposed(vt_ref, v):
    chunk = vt_ref.shape[3]
    for c in range(vt_ref.shape[1]):
        vt_ref[0, c] = v[c * chunk:(c + 1) * chunk, :].T.astype(BF16)


def _mla_kv(lat_b, kr_pad, wkn_ref, wv_ref, km_ref, vm_ref, transposed):
    kn = _dot(lat_b, wkn_ref[...])
    vv = _dot(lat_b, wv_ref[...])
    kr_b = kr_pad.astype(BF16)
    for h in range(MLA_HEADS):
        km_ref[0, :, h * MLA_BLK:h * MLA_BLK + MLA_NOPE] = kn[:, h * MLA_NOPE:(h + 1) * MLA_NOPE].astype(BF16)
        km_ref[0, :, h * MLA_BLK + MLA_NOPE:(h + 1) * MLA_BLK] = kr_b
    if transposed:
        _store_transposed(vm_ref, vv)
    else:
        vm_ref[0] = vv.astype(BF16)


def _ab_proj_kernel(*refs, with_kv):
    (x_ref, mod_ref, g_ref, cos_ref, sin_ref, win_ref, qg_ref, wuq_ref, kvg_ref, wkn_ref, wv_ref,
     aq_ref, qm_ref, ak_ref, av_ref, lat_ref, kr_ref) = refs[:17]
    nb, s, d = x_ref.shape
    rows = nb * s
    h = _modulated(x_ref[...], g_ref[1:2, :], mod_ref[:, 3], mod_ref[:, 4])
    h = h.reshape(rows, d).astype(BF16)
    cos = cos_ref[...]
    sin = sin_ref[...]
    lo32 = (lax.broadcasted_iota(jnp.int32, (rows, LANES), 1) & (CHUNK - 1)) < (CHUNK // 2)

    def rope(t):
        sw = jnp.where(lo32, pltpu.roll(t, LANES - CHUNK // 2, 1), pltpu.roll(t, CHUNK // 2, 1))
        return t.reshape(nb, s, LANES) * cos + sw.reshape(nb, s, LANES) * sin

    a_qk = A_HEADS * 2 * A_HEAD_DIM
    a_v = A_HEADS * A_V_DIM
    q_rank = qg_ref.shape[1]
    kv_rank = kvg_ref.shape[1]
    c0 = 0
    aq = _dot(h, win_ref[:, c0:c0 + a_qk])
    c0 += a_qk
    ak = _dot(h, win_ref[:, c0:c0 + a_qk])
    c0 += a_qk
    for b in range(a_qk // LANES):
        sl = slice(b * LANES, (b + 1) * LANES)
        aq_ref[:, :, sl] = (rope(aq[:, sl]) * A_SCALE).astype(BF16)
        kb = rope(ak[:, sl])
        ak_ref[:, :, sl] = kb
        if with_kv:
            refs[17][:, :, sl] = kb.astype(BF16)
    av = _dot(h, win_ref[:, c0:c0 + a_v])
    c0 += a_v
    av_ref[...] = av.reshape(nb, s, a_v)
    if with_kv:
        _store_transposed(refs[18], av)
    cq = _dot(h, win_ref[:, c0:c0 + q_rank])
    c0 += q_rank
    cqn = (_rms_scale(cq) * qg_ref[...]).astype(BF16)
    bq = _dot(cqn, wuq_ref[...])
    for hh in range(MLA_HEADS):
        nope = bq[:, hh * MLA_NOPE:(hh + 1) * MLA_NOPE] * MLA_SCALE
        qm_ref[:, :, hh * MLA_BLK:hh * MLA_BLK + MLA_NOPE] = nope.astype(BF16).reshape(nb, s, MLA_NOPE)
        off = MLA_HEADS * MLA_NOPE + hh * LANES
        qm_ref[:, :, hh * MLA_BLK + MLA_NOPE:(hh + 1) * MLA_BLK] = (
            rope(bq[:, off:off + LANES]) * MLA_SCALE).astype(BF16)
    ckv = _dot(h, win_ref[:, c0:c0 + kv_rank])
    c0 += kv_rank
    lat = _rms_scale(ckv) * kvg_ref[...]
    lat_ref[...] = lat.reshape(nb, s, kv_rank)
    kr_pad = rope(_dot(h, win_ref[:, c0:c0 + LANES]))
    kr_ref[...] = kr_pad[:, :, :MLA_ROPE]
    if with_kv:
        _mla_kv(lat.astype(BF16), kr_pad.reshape(rows, LANES), wkn_ref, wv_ref, refs[19], refs[20], True)


def _ab_proj(x, mod, g, cos, sin, w, *, nb, s, kv_chunk=None):
    bsz, length, d = x.shape
    a_qk = A_HEADS * 2 * A_HEAD_DIM
    a_v = A_HEADS * A_V_DIM
    kv_rank = w["kvg"].shape[1]
    m_w = MLA_HEADS * MLA_BLK
    with_kv = kv_chunk is not None

    def rowspec(n):
        return pl.BlockSpec((nb, s, n), lambda i, j: (i, j, 0))

    outs = [(a_qk, BF16), (m_w, BF16), (a_qk, F32), (a_v, F32), (kv_rank, F32), (MLA_ROPE, F32)]
    out_specs = [rowspec(n) for n, _ in outs]
    out_shape = [jax.ShapeDtypeStruct((bsz, length, n), dt) for n, dt in outs]
    if with_kv:
        assert nb == 1 and s % kv_chunk == 0
        per = s // kv_chunk

        def vtspec(n):
            return pl.BlockSpec((1, per, n, kv_chunk), lambda i, j: (i, j, 0, 0))

        def vtshape(n):
            return jax.ShapeDtypeStruct((bsz, length // kv_chunk, n, kv_chunk), BF16)

        out_specs += [rowspec(a_qk), vtspec(a_v), rowspec(m_w), vtspec(MLA_HEADS * MLA_V)]
        out_shape += [jax.ShapeDtypeStruct((bsz, length, a_qk), BF16), vtshape(a_v),
                      jax.ShapeDtypeStruct((bsz, length, m_w), BF16), vtshape(MLA_HEADS * MLA_V)]
    return pl.pallas_call(
        functools.partial(_ab_proj_kernel, with_kv=with_kv),
        grid=(bsz // nb, length // s),
        in_specs=[rowspec(d), _mod_spec(nb, d), _vmem_spec(),
                  pl.BlockSpec((s, LANES), lambda i, j: (j, 0)),
                  pl.BlockSpec((s, LANES), lambda i, j: (j, 0)),
                  _vmem_spec(), _vmem_spec(), _vmem_spec(), _vmem_spec(), _vmem_spec(), _vmem_spec()],
        out_specs=out_specs,
        out_shape=out_shape,
        compiler_params=_params(2),
        name="ab_proj",
    )(x, mod, g, cos, sin, w["win"], w["qg"], w["wuq"], w["kvg"], w["wkn"], w["wv"])


def _mla_kv_kernel(lat_ref, kr_ref, wkn_ref, wv_ref, km_ref, vm_ref):
    _mla_kv(lat_ref[0].astype(BF16), kr_ref[0], wkn_ref, wv_ref, km_ref, vm_ref, False)


def _mla_kv_cache(lat_all, kr_all, w):
    bsz, lk, r = lat_all.shape
    m_w = MLA_HEADS * MLA_BLK
    v_w = MLA_HEADS * MLA_V
    return pl.pallas_call(
        _mla_kv_kernel,
        grid=(bsz,),
        in_specs=[pl.BlockSpec((1, lk, r), lambda i: (i, 0, 0)),
                  pl.BlockSpec((1, lk, LANES), lambda i: (i, 0, 0)),
                  _vmem_spec(), _vmem_spec()],
        out_specs=[pl.BlockSpec((1, lk, m_w), lambda i: (i, 0, 0)),
                   pl.BlockSpec((1, lk, v_w), lambda i: (i, 0, 0))],
        out_shape=[jax.ShapeDtypeStruct((bsz, lk, m_w), BF16),
                   jax.ShapeDtypeStruct((bsz, lk, v_w), BF16)],
        compiler_params=_params(1),
        name="mla_kv_cache",
    )(lat_all, kr_all, w["wkn"], w["wv"])


def _ab_attn_kernel(x_ref, mod_ref, aq_ref, ak_ref, avt_ref, qm_ref, km_ref, vmt_ref, lam_ref, sgt_ref,
                    wot_ref, o_ref, catt_ref, m_ref, l_ref, acc_ref, *, q_pos0, lk_valid, lam_init):
    _, tq, d = x_ref.shape
    tk = avt_ref.shape[3]
    q_first = q_pos0 + pl.program_id(1) * tq
    vis_end_first = jnp.left_shift(_chunk_of(q_first) + 1, CHUNK_SHIFT)
    vis_end_last = jnp.left_shift(_chunk_of(q_first + (tq - 1)) + 1, CHUNK_SHIFT)
    n_full = lax.div(jnp.minimum(vis_end_first, lk_valid), tk)
    n_vis = lax.div(jnp.minimum(vis_end_last, lk_valid) + (tk - 1), tk)
    lo64 = _lane_lo64((tq, A_V_DIM))
    zero = jnp.zeros((tq, A_V_DIM), BF16)
    m_ref[...] = jnp.full(m_ref.shape, -jnp.inf, F32)
    l_ref[...] = jnp.zeros(l_ref.shape, F32)
    acc_ref[...] = jnp.zeros(acc_ref.shape, F32)

    def update(idx, s, vt):
        m_old = m_ref[idx]
        m_new = jnp.maximum(m_old, jnp.max(s, axis=0, keepdims=True))
        alpha = jnp.exp2(m_old - m_new)
        e = jnp.exp2(s - m_new)
        l_ref[idx] = alpha * l_ref[idx] + jnp.sum(e, axis=0, keepdims=True)
        acc_ref[idx] = alpha * acc_ref[idx] + _dot(vt, e.astype(BF16))
        m_ref[idx] = m_new

    def tile(t, masked):
        k0 = pl.multiple_of(t * tk, tk)
        rows = pl.ds(k0, tk)
        if masked:
            k_idx = k0 + lax.broadcasted_iota(jnp.int32, (tk, tq), 0)
            q_pos = q_first + lax.broadcasted_iota(jnp.int32, (tk, tq), 1)
            visible = (_chunk_of(k_idx) <= _chunk_of(q_pos)) & (k_idx < lk_valid)
        for h in range(A_HEADS):
            sl = slice(h * A_V_DIM, (h + 1) * A_V_DIM)
            wsl = slice(h * MLA_BLK, (h + 1) * MLA_BLK)
            qp = aq_ref[0, :, sl]
            kp = ak_ref[0, rows, sl]
            scores = [_dot_t(kp, jnp.where(lo64, qp, zero)),
                      _dot_t(kp, jnp.where(lo64, zero, qp)),
                      _dot_t(km_ref[0, rows, wsl], qm_ref[0, :, wsl])]
            if masked:
                scores = [jnp.where(visible, s, -jnp.inf) for s in scores]
            vt = avt_ref[0, t, sl, :]
            update(3 * h, scores[0], vt)
            update(3 * h + 1, scores[1], vt)
            update(3 * h + 2, scores[2], vmt_ref[0, t, h * MLA_V:(h + 1) * MLA_V, :])

    def full_tile(t, carry):
        tile(t, False)
        return carry

    def masked_tile(t, carry):
        tile(t, True)
        return carry

    lax.fori_loop(0, n_full, full_tile, 0)
    lax.fori_loop(n_full, n_vis, masked_tile, 0)

    lam = _diff_lambda(lam_ref, lam_init)

    def normalized(idx):
        return acc_ref[idx] * (1.0 / l_ref[idx])

    for h in range(A_HEADS):
        oa = normalized(3 * h) - lam * normalized(3 * h + 1)
        oa = oa * lax.rsqrt(jnp.mean(oa * oa, axis=0, keepdims=True) + EPS) * sgt_ref[...] * (1.0 - lam_init)
        catt_ref[h * A_V_DIM:(h + 1) * A_V_DIM, :] = oa.astype(BF16)
        off = A_HEADS * A_V_DIM + h * MLA_V
        catt_ref[off:off + MLA_V, :] = normalized(3 * h + 2).astype(BF16)
    mix_t = _dot(wot_ref[...], catt_ref[...])
    o_ref[0] = x_ref[0] + mod_ref[0, 5] * mix_t.T


def _ab_attn(x, mod, aq, ak, avt, qm, km, vmt, lam_p, sgt, wot, *, tq, q_pos0, lk_valid, lam_init):
    bsz, lq, d = x.shape
    lk = ak.shape[1]
    n_streams = 3 * A_HEADS

    def qspec(n):
        return pl.BlockSpec((1, tq, n), lambda i, j: (i, j, 0))

    def kspec(n):
        return pl.BlockSpec((1, lk, n), lambda i, j: (i, 0, 0))

    def vtspec(a):
        return pl.BlockSpec((1,) + a.shape[1:], lambda i, j: (i, 0, 0, 0))

    return pl.pallas_call(
        functools.partial(_ab_attn_kernel, q_pos0=q_pos0, lk_valid=lk_valid, lam_init=lam_init),
        grid=(bsz, lq // tq),
        in_specs=[qspec(d), _mod_spec(1, d),
                  qspec(aq.shape[2]), kspec(ak.shape[2]), vtspec(avt),
                  qspec(qm.shape[2]), kspec(km.shape[2]), vtspec(vmt),
                  _vmem_spec(), _vmem_spec(), _vmem_spec()],
        out_specs=qspec(d),
        out_shape=jax.ShapeDtypeStruct(x.shape, F32),
        scratch_shapes=[pltpu.VMEM((wot.shape[1], tq), BF16),
                        pltpu.VMEM((n_streams, 1, tq), F32),
                        pltpu.VMEM((n_streams, 1, tq), F32),
                        pltpu.VMEM((n_streams, A_V_DIM, tq), F32)],
        compiler_params=_params(2),
        name="ab_attn",
    )(x, mod, aq, ak, avt, qm, km, vmt, lam_p, sgt, wot)


def _ab_attn_rows_kernel(x_ref, mod_ref, aq_ref, ak_ref, av_ref, qm_ref, km_ref, vm_ref, lam_ref, sg_ref,
                         wo_ref, o_ref, cat_ref, *, q_pos0, lk_valid, lam_init):
    _, tq, d = x_ref.shape
    lk = ak_ref.shape[1]
    q_pos = q_pos0 + pl.program_id(1) * tq + lax.broadcasted_iota(jnp.int32, (tq, lk), 0)
    k_idx = lax.broadcasted_iota(jnp.int32, (tq, lk), 1)
    visible = (_chunk_of(k_idx) <= _chunk_of(q_pos)) & (k_idx < lk_valid)
    lam = _diff_lambda(lam_ref, lam_init)
    lo64 = _lane_lo64((tq, A_V_DIM))
    zero = jnp.zeros((tq, A_V_DIM), BF16)

    def softmax_parts(s):
        s = jnp.where(visible, s, -jnp.inf)
        e = jnp.exp2(s - jnp.max(s, axis=-1, keepdims=True))
        return e, 1.0 / jnp.sum(e, axis=-1, keepdims=True)

    for h in range(A_HEADS):
        sl = slice(h * A_V_DIM, (h + 1) * A_V_DIM)
        qp = aq_ref[0, :, sl]
        kp = ak_ref[0, :, sl]
        e0, r0 = softmax_parts(_dot_t(jnp.where(lo64, qp, zero), kp))
        e1, r1 = softmax_parts(_dot_t(jnp.where(lo64, zero, qp), kp))
        w = (e0 * r0 - e1 * (lam * r1)).astype(BF16)
        oa = _dot(w, av_ref[0, :, sl])
        oa = _rms_scale(oa) * sg_ref[...] * (1.0 - lam_init)
        cat_ref[:, sl] = oa.astype(BF16)
        msl = slice(h * MLA_BLK, (h + 1) * MLA_BLK)
        e, r = softmax_parts(_dot_t(qm_ref[0, :, msl], km_ref[0, :, msl]))
        ob = _dot((e * r).astype(BF16), vm_ref[0, :, h * MLA_V:(h + 1) * MLA_V])
        off = A_HEADS * A_V_DIM + h * MLA_V
        cat_ref[:, off:off + MLA_V] = ob.astype(BF16)
    mix = _dot(cat_ref[...], wo_ref[...])
    o_ref[0] = x_ref[0] + mod_ref[0, 5] * mix


def _ab_attn_rows(x, mod, aq, ak, av, qm, km, vm, lam_p, sg, wo, *, q_pos0, lk_valid, lam_init):
    bsz, tq, d = x.shape
    lk = ak.shape[1]

    def qspec(n):
        return pl.BlockSpec((1, tq, n), lambda i, j: (i, 0, 0))

    def kspec(n):
        return pl.BlockSpec((1, lk, n), lambda i, j: (i, 0, 0))

    return pl.pallas_call(
        functools.partial(_ab_attn_rows_kernel, q_pos0=q_pos0, lk_valid=lk_valid, lam_init=lam_init),
        grid=(bsz, 1),
        in_specs=[qspec(d), _mod_spec(1, d),
                  qspec(aq.shape[2]), kspec(ak.shape[2]), kspec(av.shape[2]),
                  qspec(qm.shape[2]), kspec(km.shape[2]), kspec(vm.shape[2]),
                  _vmem_spec(), _vmem_spec(), _vmem_spec()],
        out_specs=qspec(d),
        out_shape=jax.ShapeDtypeStruct(x.shape, F32),
        scratch_shapes=[pltpu.VMEM((tq, wo.shape[0]), BF16)],
        compiler_params=_params(2),
        name="ab_attn_sample",
    )(x, mod, aq, ak, av, qm, km, vm, lam_p, sg, wo)


def _c_proj_kernel(x_ref, mod_ref, g_ref, win_ref, q_ref, kp_ref, vt_ref, ck_ref, cv_ref):
    j = pl.program_id(1)
    hd = C_HEADS * C_HEAD_DIM

    @pl.when(j == 0)
    def _():
        kp_ref[...] = jnp.zeros(kp_ref.shape, BF16)
        vt_ref[...] = jnp.zeros(vt_ref.shape, BF16)

    @pl.when(j > 0)
    def _():
        _, s, d = x_ref.shape
        h = _modulated(x_ref[...], g_ref[1:2, :], mod_ref[:, 3], mod_ref[:, 4])
        h = h.reshape(s, d).astype(BF16)
        q_ref[0] = (_dot(h, win_ref[:, :hd]) * C_SCALE).astype(BF16)
        k = _dot(h, win_ref[:, hd:2 * hd])
        v = _dot(h, win_ref[:, 2 * hd:])
        kp_ref[0] = k.astype(BF16)
        _store_transposed(vt_ref, v)

        @pl.when(j == pl.num_programs(1) - 1)
        def _():
            ck_ref[0] = k
            cv_ref[0] = v


def _c_proj(x, mod, g, win, *, v_chunk):
    bsz, length, d = x.shape
    hd = C_HEADS * C_HEAD_DIM
    s = C_BAND_PAST
    nj = length // s
    per = s // v_chunk

    def prev(i, j):
        return (i, jnp.maximum(j - 1, 0), 0)

    return pl.pallas_call(
        _c_proj_kernel,
        grid=(bsz, nj + 1),
        in_specs=[pl.BlockSpec((1, s, d), prev), _mod_spec(1, d), _vmem_spec(), _vmem_spec()],
        out_specs=[pl.BlockSpec((1, s, hd), prev),
                   pl.BlockSpec((1, s, hd), lambda i, j: (i, j, 0)),
                   pl.BlockSpec((1, per, hd, v_chunk), lambda i, j: (i, j, 0, 0)),
                   pl.BlockSpec((1, s, hd), lambda i, j: (i, 0, 0)),
                   pl.BlockSpec((1, s, hd), lambda i, j: (i, 0, 0))],
        out_shape=[jax.ShapeDtypeStruct((bsz, length, hd), BF16),
                   jax.ShapeDtypeStruct((bsz, length + s, hd), BF16),
                   jax.ShapeDtypeStruct((bsz, (length + s) // v_chunk, hd, v_chunk), BF16),
                   jax.ShapeDtypeStruct((bsz, s, hd), F32),
                   jax.ShapeDtypeStruct((bsz, s, hd), F32)],
        compiler_params=_params(2),
        name="c_proj",
    )(x, mod, g, win)


def _modproj_kernel(x_ref, mod_ref, g_ref, win_ref, o_ref):
    nb, s, d = x_ref.shape
    h = _modulated(x_ref[...], g_ref[1:2, :], mod_ref[:, 3], mod_ref[:, 4])
    h = h.reshape(nb * s, d).astype(BF16)
    o_ref[...] = _dot(h, win_ref[...]).reshape(o_ref.shape)


def _modproj(x, mod, g, win):
    bsz, s, d = x.shape
    n = win.shape[1]
    return pl.pallas_call(
        _modproj_kernel,
        grid=(1, 1),
        in_specs=[pl.BlockSpec((bsz, s, d), lambda i, j: (0, 0, 0)), _mod_spec(bsz, d),
                  _vmem_spec(), _vmem_spec()],
        out_specs=pl.BlockSpec((bsz, s, n), lambda i, j: (0, 0, 0)),
        out_shape=jax.ShapeDtypeStruct((bsz, s, n), F32),
        compiler_params=_params(2),
        name="c_proj_sample",
    )(x, mod, g, win)


def _band_table(rb_ref, hh, tq, window):
    a_chunk = _chunk_of(lax.broadcasted_iota(jnp.int32, (tq, window), 0))
    c_chunk = _chunk_of(lax.broadcasted_iota(jnp.int32, (tq, window), 1))
    in_band = (c_chunk >= a_chunk) & (c_chunk - C_PAST_CHUNKS <= a_chunk)
    row = jnp.broadcast_to(rb_ref[hh:hh + 1, :], (tq, rb_ref.shape[1]))
    toeplitz = pltpu.roll(row, 0, 1, stride=1, stride_axis=0)[:, :window]
    return jnp.where(in_band, toeplitz, -jnp.inf)


def _c_attn_kernel(x_ref, mod_ref, q_ref, k_ref, vt_ref, rb_ref, wot_ref, o_ref, catt_ref, tbt_ref, *,
                   window, pos_base):
    _, tq, d = x_ref.shape
    v_chunk = vt_ref.shape[3]
    n_chunks = window // v_chunk

    @pl.when((pl.program_id(0) == 0) & (pl.program_id(1) == 0))
    def _():
        for hh in range(C_HEADS):
            tbt_ref[hh] = _band_table(rb_ref, hh, tq, window).T

    j = pl.program_id(1)
    q0 = pl.multiple_of(j * tq, tq)
    rows = pl.ds(q0, window)
    chunk0 = j * (tq // v_chunk)
    lo64 = _lane_lo64((tq, LANES))
    zero = jnp.zeros((tq, LANES), BF16)
    half = C_HEAD_DIM

    visible = (pos_base + q0 + lax.broadcasted_iota(jnp.int32, (window, tq), 0)) >= 0
    for p in range(C_HEADS // 2):
        sl = slice(p * LANES, (p + 1) * LANES)
        qp = q_ref[0, :, sl]
        kp = k_ref[0, rows, sl]
        for t in range(2):
            qt = jnp.where(lo64, qp, zero) if t == 0 else jnp.where(lo64, zero, qp)
            s = _dot_t(kp, qt) + tbt_ref[2 * p + t]
            s = jnp.where(visible, s, -jnp.inf)
            e = jnp.exp2(s - jnp.max(s, axis=0, keepdims=True))
            inv = 1.0 / jnp.sum(e, axis=0, keepdims=True)
            eb = e.astype(BF16)
            acc = _dot(vt_ref[0, chunk0, sl, :], eb[:v_chunk, :])
            for c in range(1, n_chunks):
                acc = acc + _dot(vt_ref[0, chunk0 + c, sl, :], eb[c * v_chunk:(c + 1) * v_chunk, :])
            r0 = p * LANES + t * half
            catt_ref[r0:r0 + half, :] = (acc[t * half:(t + 1) * half, :] * inv).astype(BF16)
    mix_t = _dot(wot_ref[...], catt_ref[...])
    o_ref[0] = x_ref[0] + mod_ref[0, 5] * mix_t.T


def _c_attn(x, mod, q, kpad, vt, rb, wot, *, tq, window, pos_base):
    bsz, lq, d = x.shape
    lkp = kpad.shape[1]
    hd = q.shape[2]
    assert tq % vt.shape[3] == 0 and window % vt.shape[3] == 0

    def qspec(n):
        return pl.BlockSpec((1, tq, n), lambda i, j: (i, j, 0))

    return pl.pallas_call(
        functools.partial(_c_attn_kernel, window=window, pos_base=pos_base),
        grid=(bsz, lq // tq),
        in_specs=[qspec(d), _mod_spec(1, d), qspec(hd),
                  pl.BlockSpec((1, lkp, hd), lambda i, j: (i, 0, 0)),
                  pl.BlockSpec((1,) + vt.shape[1:], lambda i, j: (i, 0, 0, 0)),
                  _vmem_spec(), _vmem_spec()],
        out_specs=qspec(d),
        out_shape=jax.ShapeDtypeStruct(x.shape, F32),
        scratch_shapes=[pltpu.VMEM((hd, tq), BF16),
                        pltpu.VMEM((C_HEADS, window, tq), F32)],
        compiler_params=_params(2),
        name="c_attn",
    )(x, mod, q, kpad, vt, rb, wot)


def _c_attn_rows_kernel(x_ref, mod_ref, q_ref, k_ref, v_ref, rb_ref, wo_ref, o_ref, cat_ref, *,
                        window, lk_valid):
    _, tq, d = x_ref.shape
    visible = lax.broadcasted_iota(jnp.int32, (tq, window), 1) < lk_valid
    lo64 = _lane_lo64((tq, LANES))
    zero = jnp.zeros((tq, LANES), BF16)
    for p in range(C_HEADS // 2):
        sl = slice(p * LANES, (p + 1) * LANES)
        qp = q_ref[0, :, sl]
        kp = k_ref[0, :, sl]
        vp = v_ref[0, :, sl]
        halves = []
        for t in range(2):
            qt = jnp.where(lo64, qp, zero) if t == 0 else jnp.where(lo64, zero, qp)
            s = _dot_t(qt, kp) + _band_table(rb_ref, 2 * p + t, tq, window)
            s = jnp.where(visible, s, -jnp.inf)
            e = jnp.exp2(s - jnp.max(s, axis=-1, keepdims=True))
            inv = 1.0 / jnp.sum(e, axis=-1, keepdims=True)
            halves.append(_dot(e.astype(BF16), vp) * inv)
        cat_ref[:, sl] = jnp.where(lo64, halves[0], halves[1]).astype(BF16)
    mix = _dot(cat_ref[...], wo_ref[...])
    o_ref[0] = x_ref[0] + mod_ref[0, 5] * mix


def _c_attn_rows(x, mod, q, kpad, vpad, rb, wo, *, lk_valid):
    bsz, tq, d = x.shape
    window = kpad.shape[1]
    hd = q.shape[2]

    def spec(rows, n):
        return pl.BlockSpec((1, rows, n), lambda i, j: (i, 0, 0))

    return pl.pallas_call(
        functools.partial(_c_attn_rows_kernel, window=window, lk_valid=lk_valid),
        grid=(bsz, 1),
        in_specs=[spec(tq, d), _mod_spec(1, d), spec(tq, hd), spec(window, hd), spec(window, hd),
                  _vmem_spec(), _vmem_spec()],
        out_specs=spec(tq, d),
        out_shape=jax.ShapeDtypeStruct(x.shape, F32),
        scratch_shapes=[pltpu.VMEM((tq, hd), BF16)],
        compiler_params=_params(2),
        name="c_attn_sample",
    )(x, mod, q, kpad, vpad, rb, wo)


def _rope_tables(pos0, length):
    half = CHUNK // 2
    inv = 1.0 / (ROPE_THETA ** (jnp.arange(half, dtype=F32) * (2.0 / CHUNK)))
    ang = (pos0 + jnp.arange(length, dtype=jnp.int32)).astype(F32)[:, None] * inv[None, :]
    cos = jnp.cos(ang)
    sin = jnp.sin(ang)
    cos = jnp.concatenate([cos, cos, cos, cos], axis=-1)
    sin = jnp.concatenate([-sin, sin, -sin, sin], axis=-1)
    return cos, sin


def _band_bias_rows(rel_bias, tq, window):
    heads, n_rel = rel_bias.shape
    roll_w = _round_up(window + tq, LANES)
    lead = C_BAND_PAST - REL_CLIP
    n_mid = min(n_rel, window - lead)
    n_hi = window - lead - n_mid

    def rep(col, n):
        return jnp.broadcast_to(rel_bias[:, col:col + 1], (heads, n))

    rows = jnp.concatenate([rep(0, lead), rel_bias[:, :n_mid], rep(n_rel - 1, n_hi),
                            rep(0, roll_w - window)], axis=1)
    return rows.astype(F32) * LOG2E


def _pad_rows(a, rows):
    return jnp.pad(a, ((0, 0), (0, rows - a.shape[1]), (0, 0)))


def _round_up(n, m):
    return (n + m - 1) // m * m


def _prep_weights(ffn_w_in, ffn_w_out, ab_w_in, mla_q_norm_g, mla_w_uq, mla_kv_norm_g, mla_w_ukv,
                  ab_w_out, c_w_in, c_w_out):
    w = {"ffn_in": ffn_w_in.astype(BF16), "ffn_out": ffn_w_out.astype(BF16), "ab": [], "c": []}
    for i in range(ab_w_in.shape[0]):
        win = ab_w_in[i]
        pad = _round_up(win.shape[1], LANES) - win.shape[1]
        win = jnp.pad(win, ((0, 0), (0, pad))).astype(BF16)
        q_rank = mla_w_uq.shape[1]
        uq = mla_w_uq[i].reshape(q_rank, MLA_HEADS, MLA_NOPE + MLA_ROPE)
        uq_rope = jnp.pad(uq[:, :, MLA_NOPE:], ((0, 0), (0, 0), (0, LANES - MLA_ROPE)))
        wuq = jnp.concatenate([uq[:, :, :MLA_NOPE].reshape(q_rank, -1), uq_rope.reshape(q_rank, -1)], axis=1)
        kv_rank = mla_w_ukv.shape[1]
        ukv = mla_w_ukv[i].reshape(kv_rank, MLA_HEADS, MLA_NOPE + MLA_V)
        wo = ab_w_out[i].astype(BF16)
        w["ab"].append({
            "win": win,
            "qg": mla_q_norm_g[i].reshape(1, -1),
            "wuq": wuq.astype(BF16),
            "kvg": mla_kv_norm_g[i].reshape(1, -1),
            "wkn": ukv[:, :, :MLA_NOPE].reshape(kv_rank, -1).astype(BF16),
            "wv": ukv[:, :, MLA_NOPE:].reshape(kv_rank, -1).astype(BF16),
            "wo": wo, "wot": wo.T,
        })
    for i in range(c_w_in.shape[0]):
        wo = c_w_out[i].astype(BF16)
        w["c"].append({"win": c_w_in[i].astype(BF16), "wo": wo, "wot": wo.T})
    return w


def _trunk(x, mod, past, prm, w, *, prompt):
    bsz, length, d = x.shape
    depth = mod.shape[0]
    if prompt:
        nb, s = 1, min(ROW_TILE, length)
        pos0 = 0
    else:
        nb, s = bsz, length
        pos0 = past["a_k"].shape[2]
    cos, sin = _rope_tables(pos0, length)
    ab_rows, c_rows = [], []
    for l in range(depth):
        g = prm["norm_g"][l]
        x = _ffn(x, mod[l], g, w["ffn_in"][l, 0], w["ffn_out"][l, 0], k0=0, g_row=0, nb=nb, s=s)
        i = l // 2
        if l % 2 == 0:
            wl = w["ab"][i]
            lam_init = 0.8 - 0.6 * math.exp(-0.3 * l)
            sg = prm["a_subln_g"][i]
            if prompt:
                aq, qm, ak, av, lat, kr, akb, avt, km, vmt = _ab_proj(
                    x, mod[l], g, cos, sin, wl, nb=nb, s=s, kv_chunk=min(K_TILE_AB, length))
                x = _ab_attn(x, mod[l], aq, akb, avt, qm, km, vmt, prm["a_lambda"][i], sg.reshape(-1, 1),
                             wl["wot"], tq=min(Q_TILE_AB, length), q_pos0=0, lk_valid=length,
                             lam_init=lam_init)
            else:
                aq, qm, ak, av, lat, kr = _ab_proj(x, mod[l], g, cos, sin, wl, nb=nb, s=s)
                lk_valid = pos0 + length
                lk = _round_up(lk_valid, LANES)
                pa_k = past["a_k"][i].reshape(bsz, pos0, -1)
                pa_v = past["a_v"][i].reshape(bsz, pos0, -1)
                akb = _pad_rows(jnp.concatenate([pa_k, ak], axis=1).astype(BF16), lk)
                avb = _pad_rows(jnp.concatenate([pa_v, av], axis=1).astype(BF16), lk)
                lat_all = _pad_rows(jnp.concatenate([past["mla_latent"][i], lat], axis=1), lk)
                kr_all = jnp.concatenate([past["mla_krope"][i], kr], axis=1)
                kr_all = jnp.pad(kr_all, ((0, 0), (0, lk - lk_valid), (0, LANES - MLA_ROPE)))
                km, vm = _mla_kv_cache(lat_all, kr_all, wl)
                x = _ab_attn_rows(x, mod[l], aq, akb, avb, qm, km, vm, prm["a_lambda"][i], sg.reshape(1, -1),
                                  wl["wo"], q_pos0=pos0, lk_valid=lk_valid, lam_init=lam_init)
            ab_rows.append((ak.reshape(bsz, length, A_HEADS, 2, A_HEAD_DIM),
                            av.reshape(bsz, length, A_HEADS, A_V_DIM), lat, kr))
        else:
            wl = w["c"][i]
            hd = C_HEADS * C_HEAD_DIM
            if prompt:
                tq = min(Q_TILE_C, length)
                window = C_BAND_PAST + tq
                q, kpad, vt, ck, cv = _c_proj(x, mod[l], g, wl["win"], v_chunk=tq)
                rb = _band_bias_rows(prm["c_rel_bias"][i], tq, window)
                x = _c_attn(x, mod[l], q, kpad, vt, rb, wl["wot"], tq=tq, window=window,
                            pos_base=-C_BAND_PAST)
            else:
                pk = past["c_k"][i].reshape(bsz, -1, hd)
                pv = past["c_v"][i].reshape(bsz, -1, hd)
                lc = pk.shape[1]
                qkv = _modproj(x, mod[l], g, wl["win"])
                q = (qkv[:, :, :hd] * C_SCALE).astype(BF16)
                k_all = jnp.concatenate([pk, qkv[:, :, hd:2 * hd]], axis=1)
                v_all = jnp.concatenate([pv, qkv[:, :, 2 * hd:]], axis=1)
                window = _round_up(C_BAND_PAST + length, LANES)
                rb = _band_bias_rows(prm["c_rel_bias"][i], length, window)
                x = _c_attn_rows(x, mod[l], q, _pad_rows(k_all.astype(BF16), window),
                                 _pad_rows(v_all.astype(BF16), window), rb, wl["wo"], lk_valid=lc + length)
                ck, cv = k_all[:, -lc:], v_all[:, -lc:]
            c_rows.append((ck.reshape(bsz, -1, C_HEADS, C_HEAD_DIM), cv.reshape(bsz, -1, C_HEADS, C_HEAD_DIM)))
        x = _ffn(x, mod[l], g, w["ffn_in"][l, 1], w["ffn_out"][l, 1], k0=6, g_row=2, nb=nb, s=s,
                 final_g=prm["final_norm_g"] if l == depth - 1 else None)
    ab_state = tuple(jnp.stack([r[j] for r in ab_rows]) for j in range(4))
    c_state = tuple(jnp.stack([r[j] for r in c_rows]) for j in range(2))
    return x, ab_state, c_state


def kernel(x_prompt, x_sample, cache_a_k, cache_a_v, cache_mla_latent, cache_mla_krope, cache_c_k, cache_c_v,
           c_prompt, c_sample, ada_w, ada_b, norm_g, ffn_w_in, ffn_w_out, ab_w_in, a_lambda, a_subln_g,
           mla_q_norm_g, mla_w_uq, mla_kv_norm_g, mla_w_ukv, ab_w_out, c_w_in, c_rel_bias, c_w_out,
           final_norm_g):
    bp = x_prompt.shape[0]
    d = x_prompt.shape[2]
    depth = ada_w.shape[0]
    c_all = jnp.concatenate([c_prompt, c_sample], axis=0)
    n_seq = c_all.shape[0]
    c_all = jnp.pad(c_all, ((0, _round_up(n_seq, 16) - n_seq), (0, 0)))
    mod = _modulation(c_all, ada_w, ada_b)[:, :n_seq]
    mod = mod.reshape(depth, n_seq, N_MOD, 1, d)
    w = _prep_weights(ffn_w_in, ffn_w_out, ab_w_in, mla_q_norm_g, mla_w_uq, mla_kv_norm_g, mla_w_ukv,
                      ab_w_out, c_w_in, c_w_out)
    prm = dict(norm_g=norm_g, a_lambda=a_lambda, a_subln_g=a_subln_g, c_rel_bias=c_rel_bias,
               final_norm_g=final_norm_g)
    y_p, ab_p, c_p = _trunk(x_prompt, mod[:, :bp], None, prm, w, prompt=True)
    past = dict(a_k=cache_a_k, a_v=cache_a_v, mla_latent=cache_mla_latent, mla_krope=cache_mla_krope,
                c_k=cache_c_k, c_v=cache_c_v)
    y_s, ab_s, c_s = _trunk(x_sample, mod[:, bp:], past, prm, w, prompt=False)
    return (y_p, y_s, ab_p[0], ab_s[0], ab_p[1], ab_s[1], ab_p[2], ab_s[2], ab_p[3], ab_s[3],
            c_p[0], c_s[0], c_p[1], c_s[1])
```

```python
import functools
import math

import jax
import jax.numpy as jnp
from jax import lax
from jax.experimental import pallas as pl
from jax.experimental.pallas import tpu as pltpu

F32 = jnp.float32
BF16 = jnp.bfloat16

EPS = 1e-5
CHUNK = 64
CHUNK_SHIFT = CHUNK.bit_length() - 1
ROPE_THETA = 10000.0
N_MOD = 9
LOG2E = math.log2(math.e)
A_HEADS = 4
A_HEAD_DIM = 64
A_V_DIM = 2 * A_HEAD_DIM
A_SCALE = A_HEAD_DIM ** -0.5 * LOG2E
MLA_HEADS = 4
MLA_NOPE = 128
MLA_ROPE = 64
MLA_V = 128
MLA_SCALE = (MLA_NOPE + MLA_ROPE) ** -0.5 * LOG2E
C_HEADS = 16
C_HEAD_DIM = 64
C_SCALE = C_HEAD_DIM ** -0.5 * LOG2E
C_PAST_CHUNKS = 8
C_BAND_PAST = C_PAST_CHUNKS * CHUNK
REL_CLIP = 128

LANES = 128
VMEM_LIMIT_BYTES = 56 * 1024 * 1024

ROW_TILE = 512
Q_TILE_AB = 512
K_TILE_AB = 512
Q_TILE_C = 256
FF_CHUNK = 256
MLA_BLK = MLA_NOPE + LANES


def _dot(a, b):
    return jnp.dot(a, b, preferred_element_type=F32)


def _dot_t(a, b):
    return lax.dot_general(a, b, (((1,), (1,)), ((), ())), preferred_element_type=F32)


def _rms_scale(x):
    return x * lax.rsqrt(jnp.mean(x * x, axis=-1, keepdims=True) + EPS)


def _modulated(x3, g, shift, scale):
    return _rms_scale(x3) * g * (1.0 + scale) + shift


def _lane_lo64(shape):
    return (lax.broadcasted_iota(jnp.int32, shape, len(shape) - 1) & (LANES - 1)) < CHUNK


def _chunk_of(pos):
    return jnp.right_shift(pos, CHUNK_SHIFT)


def _diff_lambda(lam_ref, lam_init):
    lf = lam_ref[...]
    return (jnp.exp(jnp.sum(lf[0:1] * lf[1:2], axis=-1, keepdims=True))
            - jnp.exp(jnp.sum(lf[2:3] * lf[3:4], axis=-1, keepdims=True)) + lam_init)


def _params(n_grid):
    return pltpu.CompilerParams(dimension_semantics=("arbitrary",) * n_grid,
                                vmem_limit_bytes=VMEM_LIMIT_BYTES)


def _vmem_spec():
    return pl.BlockSpec(memory_space=pltpu.VMEM)


def _mod_spec(nb, d):
    return pl.BlockSpec((nb, N_MOD, 1, d), lambda i, j: (i, 0, 0, 0))


def _mod_kernel(c_ref, w_ref, b_ref, o_ref):
    c = c_ref[...]
    sc = (c * jax.nn.sigmoid(c)).astype(BF16)
    o_ref[0] = _dot(sc, w_ref[0].astype(BF16)) + b_ref[0]


def _modulation(c_all, ada_w, ada_b):
    depth, d, n = ada_w.shape
    rows = c_all.shape[0]
    tn = 1536
    return pl.pallas_call(
        _mod_kernel,
        grid=(depth, n // tn),
        in_specs=[pl.BlockSpec((rows, d), lambda l, j: (0, 0)),
                  pl.BlockSpec((1, d, tn), lambda l, j: (l, 0, j)),
                  pl.BlockSpec((1, 1, tn), lambda l, j: (l, 0, j))],
        out_specs=pl.BlockSpec((1, rows, tn), lambda l, j: (l, 0, j)),
        out_shape=jax.ShapeDtypeStruct((depth, rows, n), F32),
        compiler_params=_params(2),
        name="adaln_mod",
    )(c_all, ada_w, ada_b.reshape(depth, 1, n))


def _ffn_kernel(*refs, k0, g_row, final):
    if final:
        x_ref, mod_ref, g_ref, win_ref, wout_ref, gf_ref, o_ref, act_ref = refs
    else:
        x_ref, mod_ref, g_ref, win_ref, wout_ref, o_ref, act_ref = refs
    nb, s, d = x_ref.shape
    ff = wout_ref.shape[0]
    x = x_ref[...]
    h = _modulated(x, g_ref[g_row:g_row + 1, :], mod_ref[:, k0], mod_ref[:, k0 + 1])
    h = h.reshape(nb * s, d).astype(BF16)
    for c in range(ff // FF_CHUNK):
        lo = c * FF_CHUNK
        gate = _dot(h, win_ref[:, lo:lo + FF_CHUNK])
        up = _dot(h, win_ref[:, ff + lo:ff + lo + FF_CHUNK])
        act_ref[:, lo:lo + FF_CHUNK] = (gate * jax.nn.sigmoid(gate) * up).astype(BF16)
    y = _dot(act_ref[...], wout_ref[...]).reshape(nb, s, d)
    out = x + 0.5 * mod_ref[:, k0 + 2] * y
    if final:
        out = _rms_scale(out) * gf_ref[...]
    o_ref[...] = out


def _ffn(x, mod, g, win, wout, *, k0, g_row, nb, s, final_g=None):
    bsz, length, d = x.shape
    ff = wout.shape[0]
    final = final_g is not None
    in_specs = [pl.BlockSpec((nb, s, d), lambda i, j: (i, j, 0)), _mod_spec(nb, d),
                _vmem_spec(), _vmem_spec(), _vmem_spec()]
    args = [x, mod, g, win, wout]
    if final:
        in_specs.append(_vmem_spec())
        args.append(final_g.reshape(1, d))
    return pl.pallas_call(
        functools.partial(_ffn_kernel, k0=k0, g_row=g_row, final=final),
        grid=(bsz // nb, length // s),
        in_specs=in_specs,
        out_specs=pl.BlockSpec((nb, s, d), lambda i, j: (i, j, 0)),
        out_shape=jax.ShapeDtypeStruct(x.shape, F32),
        scratch_shapes=[pltpu.VMEM((nb * s, ff), BF16)],
        compiler_params=_params(2),
        name="ffn",
    )(*args)


def _store_transposed(vt_ref, v):
    chunk = vt_ref.shape[3]
    for c in range(vt_ref.shape[1]):
        vt_ref[0, c] = v[c * chunk:(c + 1) * chunk, :].T.astype(BF16)


def _mla_kv(lat_b, kr_pad, wkn_ref, wv_ref, km_ref, vm_ref, transposed):
    kn = _dot(lat_b, wkn_ref[...])
    vv = _dot(lat_b, wv_ref[...])
    kr_b = kr_pad.astype(BF16)
    for h in range(MLA_HEADS):
        km_ref[0, :, h * MLA_BLK:h * MLA_BLK + MLA_NOPE] = kn[:, h * MLA_NOPE:(h + 1) * MLA_NOPE].astype(BF16)
        km_ref[0, :, h * MLA_BLK + MLA_NOPE:(h + 1) * MLA_BLK] = kr_b
    if transposed:
        _store_transposed(vm_ref, vv)
    else:
        vm_ref[0] = vv.astype(BF16)


def _ab_proj_kernel(*refs, with_kv):
    (x_ref, mod_ref, g_ref, cos_ref, sin_ref, win_ref, qg_ref, wuq_ref, kvg_ref, wkn_ref, wv_ref,
     aq_ref, qm_ref, ak_ref, av_ref, lat_ref, kr_ref) = refs[:17]
    nb, s, d = x_ref.shape
    rows = nb * s
    h = _modulated(x_ref[...], g_ref[1:2, :], mod_ref[:, 3], mod_ref[:, 4])
    h = h.reshape(rows, d).astype(BF16)
    cos = cos_ref[...]
    sin = sin_ref[...]
    lo32 = (lax.broadcasted_iota(jnp.int32, (rows, LANES), 1) & (CHUNK - 1)) < (CHUNK // 2)

    def rope(t):
        sw = jnp.where(lo32, pltpu.roll(t, LANES - CHUNK // 2, 1), pltpu.roll(t, CHUNK // 2, 1))
        return t.reshape(nb, s, LANES) * cos + sw.reshape(nb, s, LANES) * sin

    a_qk = A_HEADS * 2 * A_HEAD_DIM
    a_v = A_HEADS * A_V_DIM
    q_rank = qg_ref.shape[1]
    kv_rank = kvg_ref.shape[1]
    c0 = 0
    aq = _dot(h, win_ref[:, c0:c0 + a_qk])
    c0 += a_qk
    ak = _dot(h, win_ref[:, c0:c0 + a_qk])
    c0 += a_qk
    for b in range(a_qk // LANES):
        sl = slice(b * LANES, (b + 1) * LANES)
        aq_ref[:, :, sl] = (rope(aq[:, sl]) * A_SCALE).astype(BF16)
        kb = rope(ak[:, sl])
        ak_ref[:, :, sl] = kb
        if with_kv:
            refs[17][:, :, sl] = kb.astype(BF16)
    av = _dot(h, win_ref[:, c0:c0 + a_v])
    c0 += a_v
    av_ref[...] = av.reshape(nb, s, a_v)
    if with_kv:
        _store_transposed(refs[18], av)
    cq = _dot(h, win_ref[:, c0:c0 + q_rank])
    c0 += q_rank
    cqn = (_rms_scale(cq) * qg_ref[...]).astype(BF16)
    bq = _dot(cqn, wuq_ref[...])
    for hh in range(MLA_HEADS):
        nope = bq[:, hh * MLA_NOPE:(hh + 1) * MLA_NOPE] * MLA_SCALE
        qm_ref[:, :, hh * MLA_BLK:hh * MLA_BLK + MLA_NOPE] = nope.astype(BF16).reshape(nb, s, MLA_NOPE)
        off = MLA_HEADS * MLA_NOPE + hh * LANES
        qm_ref[:, :, hh * MLA_BLK + MLA_NOPE:(hh + 1) * MLA_BLK] = (
            rope(bq[:, off:off + LANES]) * MLA_SCALE).astype(BF16)
    ckv = _dot(h, win_ref[:, c0:c0 + kv_rank])
    c0 += kv_rank
    lat = _rms_scale(ckv) * kvg_ref[...]
    lat_ref[...] = lat.reshape(nb, s, kv_rank)
    kr_pad = rope(_dot(h, win_ref[:, c0:c0 + LANES]))
    kr_ref[...] = kr_pad[:, :, :MLA_ROPE]
    if with_kv:
        _mla_kv(lat.astype(BF16), kr_pad.reshape(rows, LANES), wkn_ref, wv_ref, refs[19], refs[20], True)


def _ab_proj(x, mod, g, cos, sin, w, *, nb, s, kv_chunk=None):
    bsz, length, d = x.shape
    a_qk = A_HEADS * 2 * A_HEAD_DIM
    a_v = A_HEADS * A_V_DIM
    kv_rank = w["kvg"].shape[1]
    m_w = MLA_HEADS * MLA_BLK
    with_kv = kv_chunk is not None

    def rowspec(n):
        return pl.BlockSpec((nb, s, n), lambda i, j: (i, j, 0))

    outs = [(a_qk, BF16), (m_w, BF16), (a_qk, F32), (a_v, F32), (kv_rank, F32), (MLA_ROPE, F32)]
    out_specs = [rowspec(n) for n, _ in outs]
    out_shape = [jax.ShapeDtypeStruct((bsz, length, n), dt) for n, dt in outs]
    if with_kv:
        assert nb == 1 and s % kv_chunk == 0
        per = s // kv_chunk

        def vtspec(n):
            return pl.BlockSpec((1, per, n, kv_chunk), lambda i, j: (i, j, 0, 0))

        def vtshape(n):
            return jax.ShapeDtypeStruct((bsz, length // kv_chunk, n, kv_chunk), BF16)

        out_specs += [rowspec(a_qk), vtspec(a_v), rowspec(m_w), vtspec(MLA_HEADS * MLA_V)]
        out_shape += [jax.ShapeDtypeStruct((bsz, length, a_qk), BF16), vtshape(a_v),
                      jax.ShapeDtypeStruct((bsz, length, m_w), BF16), vtshape(MLA_HEADS * MLA_V)]
    return pl.pallas_call(
        functools.partial(_ab_proj_kernel, with_kv=with_kv),
        grid=(bsz // nb, length // s),
        in_specs=[rowspec(d), _mod_spec(nb, d), _vmem_spec(),
                  pl.BlockSpec((s, LANES), lambda i, j: (j, 0)),
                  pl.BlockSpec((s, LANES), lambda i, j: (j, 0)),
                  _vmem_spec(), _vmem_spec(), _vmem_spec(), _vmem_spec(), _vmem_spec(), _vmem_spec()],
        out_specs=out_specs,
        out_shape=out_shape,
        compiler_params=_params(2),
        name="ab_proj",
    )(x, mod, g, cos, sin, w["win"], w["qg"], w["wuq"], w["kvg"], w["wkn"], w["wv"])


def _mla_kv_kernel(lat_ref, kr_ref, wkn_ref, wv_ref, km_ref, vm_ref):
    _mla_kv(lat_ref[0].astype(BF16), kr_ref[0], wkn_ref, wv_ref, km_ref, vm_ref, False)


def _mla_kv_cache(lat_all, kr_all, w):
    bsz, lk, r = lat_all.shape
    m_w = MLA_HEADS * MLA_BLK
    v_w = MLA_HEADS * MLA_V
    return pl.pallas_call(
        _mla_kv_kernel,
        grid=(bsz,),
        in_specs=[pl.BlockSpec((1, lk, r), lambda i: (i, 0, 0)),
                  pl.BlockSpec((1, lk, LANES), lambda i: (i, 0, 0)),
                  _vmem_spec(), _vmem_spec()],
        out_specs=[pl.BlockSpec((1, lk, m_w), lambda i: (i, 0, 0)),
                   pl.BlockSpec((1, lk, v_w), lambda i: (i, 0, 0))],
        out_shape=[jax.ShapeDtypeStruct((bsz, lk, m_w), BF16),
                   jax.ShapeDtypeStruct((bsz, lk, v_w), BF16)],
        compiler_params=_params(1),
        name="mla_kv_cache",
    )(lat_all, kr_all, w["wkn"], w["wv"])


def _ab_attn_kernel(x_ref, mod_ref, aq_ref, ak_ref, avt_ref, qm_ref, km_ref, vmt_ref, lam_ref, sgt_ref,
                    wot_ref, o_ref, catt_ref, m_ref, l_ref, acc_ref, *, q_pos0, lk_valid, lam_init):
    _, tq, d = x_ref.shape
    tk = avt_ref.shape[3]
    q_first = q_pos0 + pl.program_id(1) * tq
    vis_end_first = jnp.left_shift(_chunk_of(q_first) + 1, CHUNK_SHIFT)
    vis_end_last = jnp.left_shift(_chunk_of(q_first + (tq - 1)) + 1, CHUNK_SHIFT)
    n_full = lax.div(jnp.minimum(vis_end_first, lk_valid), tk)
    n_vis = lax.div(jnp.minimum(vis_end_last, lk_valid) + (tk - 1), tk)
    lo64 = _lane_lo64((tq, A_V_DIM))
    zero = jnp.zeros((tq, A_V_DIM), BF16)
    m_ref[...] = jnp.full(m_ref.shape, -jnp.inf, F32)
    l_ref[...] = jnp.zeros(l_ref.shape, F32)
    acc_ref[...] = jnp.zeros(acc_ref.shape, F32)

    def update(idx, s, vt):
        m_old = m_ref[idx]
        m_new = jnp.maximum(m_old, jnp.max(s, axis=0, keepdims=True))
        alpha = jnp.exp2(m_old - m_new)
        e = jnp.exp2(s - m_new)
        l_ref[idx] = alpha * l_ref[idx] + jnp.sum(e, axis=0, keepdims=True)
        acc_ref[idx] = alpha * acc_ref[idx] + _dot(vt, e.astype(BF16))
        m_ref[idx] = m_new

    def tile(t, masked):
        k0 = pl.multiple_of(t * tk, tk)
        rows = pl.ds(k0, tk)
        if masked:
            k_idx = k0 + lax.broadcasted_iota(jnp.int32, (tk, tq), 0)
            q_pos = q_first + lax.broadcasted_iota(jnp.int32, (tk, tq), 1)
            visible = (_chunk_of(k_idx) <= _chunk_of(q_pos)) & (k_idx < lk_valid)
        for h in range(A_HEADS):
            sl = slice(h * A_V_DIM, (h + 1) * A_V_DIM)
            wsl = slice(h * MLA_BLK, (h + 1) * MLA_BLK)
            qp = aq_ref[0, :, sl]
            kp = ak_ref[0, rows, sl]
            scores = [_dot_t(kp, jnp.where(lo64, qp, zero)),
                      _dot_t(kp, jnp.where(lo64, zero, qp)),
                      _dot_t(km_ref[0, rows, wsl], qm_ref[0, :, wsl])]
            if masked:
                scores = [jnp.where(visible, s, -jnp.inf) for s in scores]
            vt = avt_ref[0, t, sl, :]
            update(3 * h, scores[0], vt)
            update(3 * h + 1, scores[1], vt)
            update(3 * h + 2, scores[2], vmt_ref[0, t, h * MLA_V:(h + 1) * MLA_V, :])

    def full_tile(t, carry):
        tile(t, False)
        return carry

    def masked_tile(t, carry):
        tile(t, True)
        return carry

    lax.fori_loop(0, n_full, full_tile, 0)
    lax.fori_loop(n_full, n_vis, masked_tile, 0)

    lam = _diff_lambda(lam_ref, lam_init)

    def normalized(idx):
        return acc_ref[idx] * (1.0 / l_ref[idx])

    for h in range(A_HEADS):
        oa = normalized(3 * h) - lam * normalized(3 * h + 1)
        oa = oa * lax.rsqrt(jnp.mean(oa * oa, axis=0, keepdims=True) + EPS) * sgt_ref[...] * (1.0 - lam_init)
        catt_ref[h * A_V_DIM:(h + 1) * A_V_DIM, :] = oa.astype(BF16)
        off = A_HEADS * A_V_DIM + h * MLA_V
        catt_ref[off:off + MLA_V, :] = normalized(3 * h + 2).astype(BF16)
    mix_t = _dot(wot_ref[...], catt_ref[...])
    o_ref[0] = x_ref[0] + mod_ref[0, 5] * mix_t.T


def _ab_attn(x, mod, aq, ak, avt, qm, km, vmt, lam_p, sgt, wot, *, tq, q_pos0, lk_valid, lam_init):
    bsz, lq, d = x.shape
    lk = ak.shape[1]
    n_streams = 3 * A_HEADS

    def qspec(n):
        return pl.BlockSpec((1, tq, n), lambda i, j: (i, j, 0))

    def kspec(n):
        return pl.BlockSpec((1, lk, n), lambda i, j: (i, 0, 0))

    def vtspec(a):
        return pl.BlockSpec((1,) + a.shape[1:], lambda i, j: (i, 0, 0, 0))

    return pl.pallas_call(
        functools.partial(_ab_attn_kernel, q_pos0=q_pos0, lk_valid=lk_valid, lam_init=lam_init),
        grid=(bsz, lq // tq),
        in_specs=[qspec(d), _mod_spec(1, d),
                  qspec(aq.shape[2]), kspec(ak.shape[2]), vtspec(avt),
                  qspec(qm.shape[2]), kspec(km.shape[2]), vtspec(vmt),
                  _vmem_spec(), _vmem_spec(), _vmem_spec()],
        out_specs=qspec(d),
        out_shape=jax.ShapeDtypeStruct(x.shape, F32),
        scratch_shapes=[pltpu.VMEM((wot.shape[1], tq), BF16),
                        pltpu.VMEM((n_streams, 1, tq), F32),
                        pltpu.VMEM((n_streams, 1, tq), F32),
                        pltpu.VMEM((n_streams, A_V_DIM, tq), F32)],
        compiler_params=_params(2),
        name="ab_attn",
    )(x, mod, aq, ak, avt, qm, km, vmt, lam_p, sgt, wot)


def _ab_attn_rows_kernel(x_ref, mod_ref, aq_ref, ak_ref, av_ref, qm_ref, km_ref, vm_ref, lam_ref, sg_ref,
                         wo_ref, o_ref, cat_ref, *, q_pos0, lk_valid, lam_init):
    _, tq, d = x_ref.shape
    lk = ak_ref.shape[1]
    q_pos = q_pos0 + pl.program_id(1) * tq + lax.broadcasted_iota(jnp.int32, (tq, lk), 0)
    k_idx = lax.broadcasted_iota(jnp.int32, (tq, lk), 1)
    visible = (_chunk_of(k_idx) <= _chunk_of(q_pos)) & (k_idx < lk_valid)
    lam = _diff_lambda(lam_ref, lam_init)
    lo64 = _lane_lo64((tq, A_V_DIM))
    zero = jnp.zeros((tq, A_V_DIM), BF16)

    def softmax_parts(s):
        s = jnp.where(visible, s, -jnp.inf)
        e = jnp.exp2(s - jnp.max(s, axis=-1, keepdims=True))
        return e, 1.0 / jnp.sum(e, axis=-1, keepdims=True)

    for h in range(A_HEADS):
        sl = slice(h * A_V_DIM, (h + 1) * A_V_DIM)
        qp = aq_ref[0, :, sl]
        kp = ak_ref[0, :, sl]
        e0, r0 = softmax_parts(_dot_t(jnp.where(lo64, qp, zero), kp))
        e1, r1 = softmax_parts(_dot_t(jnp.where(lo64, zero, qp), kp))
        w = (e0 * r0 - e1 * (lam * r1)).astype(BF16)
        oa = _dot(w, av_ref[0, :, sl])
        oa = _rms_scale(oa) * sg_ref[...] * (1.0 - lam_init)
        cat_ref[:, sl] = oa.astype(BF16)
        msl = slice(h * MLA_BLK, (h + 1) * MLA_BLK)
        e, r = softmax_parts(_dot_t(qm_ref[0, :, msl], km_ref[0, :, msl]))
        ob = _dot((e * r).astype(BF16), vm_ref[0, :, h * MLA_V:(h + 1) * MLA_V])
        off = A_HEADS * A_V_DIM + h * MLA_V
        cat_ref[:, off:off + MLA_V] = ob.astype(BF16)
    mix = _dot(cat_ref[...], wo_ref[...])
    o_ref[0] = x_ref[0] + mod_ref[0, 5] * mix


def _ab_attn_rows(x, mod, aq, ak, av, qm, km, vm, lam_p, sg, wo, *, q_pos0, lk_valid, lam_init):
    bsz, tq, d = x.shape
    lk = ak.shape[1]

    def qspec(n):
        return pl.BlockSpec((1, tq, n), lambda i, j: (i, 0, 0))

    def kspec(n):
        return pl.BlockSpec((1, lk, n), lambda i, j: (i, 0, 0))

    return pl.pallas_call(
        functools.partial(_ab_attn_rows_kernel, q_pos0=q_pos0, lk_valid=lk_valid, lam_init=lam_init),
        grid=(bsz, 1),
        in_specs=[qspec(d), _mod_spec(1, d),
                  qspec(aq.shape[2]), kspec(ak.shape[2]), kspec(av.shape[2]),
                  qspec(qm.shape[2]), kspec(km.shape[2]), kspec(vm.shape[2]),
                  _vmem_spec(), _vmem_spec(), _vmem_spec()],
        out_specs=qspec(d),
        out_shape=jax.ShapeDtypeStruct(x.shape, F32),
        scratch_shapes=[pltpu.VMEM((tq, wo.shape[0]), BF16)],
        compiler_params=_params(2),
        name="ab_attn_sample",
    )(x, mod, aq, ak, av, qm, km, vm, lam_p, sg, wo)


def _c_proj_kernel(x_ref, mod_ref, g_ref, win_ref, q_ref, kp_ref, vt_ref, ck_ref, cv_ref):
    j = pl.program_id(1)
    hd = C_HEADS * C_HEAD_DIM

    @pl.when(j == 0)
    def _():
        kp_ref[...] = jnp.zeros(kp_ref.shape, BF16)
        vt_ref[...] = jnp.zeros(vt_ref.shape, BF16)

    @pl.when(j > 0)
    def _():
        _, s, d = x_ref.shape
        h = _modulated(x_ref[...], g_ref[1:2, :], mod_ref[:, 3], mod_ref[:, 4])
        h = h.reshape(s, d).astype(BF16)
        q_ref[0] = (_dot(h, win_ref[:, :hd]) * C_SCALE).astype(BF16)
        k = _dot(h, win_ref[:, hd:2 * hd])
        v = _dot(h, win_ref[:, 2 * hd:])
        kp_ref[0] = k.astype(BF16)
        _store_transposed(vt_ref, v)

        @pl.when(j == pl.num_programs(1) - 1)
        def _():
            ck_ref[0] = k
            cv_ref[0] = v


def _c_proj(x, mod, g, win, *, v_chunk):
    bsz, length, d = x.shape
    hd = C_HEADS * C_HEAD_DIM
    s = C_BAND_PAST
    nj = length // s
    per = s // v_chunk

    def prev(i, j):
        return (i, jnp.maximum(j - 1, 0), 0)

    return pl.pallas_call(
        _c_proj_kernel,
        grid=(bsz, nj + 1),
        in_specs=[pl.BlockSpec((1, s, d), prev), _mod_spec(1, d), _vmem_spec(), _vmem_spec()],
        out_specs=[pl.BlockSpec((1, s, hd), prev),
                   pl.BlockSpec((1, s, hd), lambda i, j: (i, j, 0)),
                   pl.BlockSpec((1, per, hd, v_chunk), lambda i, j: (i, j, 0, 0)),
                   pl.BlockSpec((1, s, hd), lambda i, j: (i, 0, 0)),
                   pl.BlockSpec((1, s, hd), lambda i, j: (i, 0, 0))],
        out_shape=[jax.ShapeDtypeStruct((bsz, length, hd), BF16),
                   jax.ShapeDtypeStruct((bsz, length + s, hd), BF16),
                   jax.ShapeDtypeStruct((bsz, (length + s) // v_chunk, hd, v_chunk), BF16),
                   jax.ShapeDtypeStruct((bsz, s, hd), F32),
                   jax.ShapeDtypeStruct((bsz, s, hd), F32)],
        compiler_params=_params(2),
        name="c_proj",
    )(x, mod, g, win)


def _modproj_kernel(x_ref, mod_ref, g_ref, win_ref, o_ref):
    nb, s, d = x_ref.shape
    h = _modulated(x_ref[...], g_ref[1:2, :], mod_ref[:, 3], mod_ref[:, 4])
    h = h.reshape(nb * s, d).astype(BF16)
    o_ref[...] = _dot(h, win_ref[...]).reshape(o_ref.shape)


def _modproj(x, mod, g, win):
    bsz, s, d = x.shape
    n = win.shape[1]
    return pl.pallas_call(
        _modproj_kernel,
        grid=(1, 1),
        in_specs=[pl.BlockSpec((bsz, s, d), lambda i, j: (0, 0, 0)), _mod_spec(bsz, d),
                  _vmem_spec(), _vmem_spec()],
        out_specs=pl.BlockSpec((bsz, s, n), lambda i, j: (0, 0, 0)),
        out_shape=jax.ShapeDtypeStruct((bsz, s, n), F32),
        compiler_params=_params(2),
        name="c_proj_sample",
    )(x, mod, g, win)


def _band_table(rb_ref, hh, tq, window):
    a_chunk = _chunk_of(lax.broadcasted_iota(jnp.int32, (tq, window), 0))
    c_chunk = _chunk_of(lax.broadcasted_iota(jnp.int32, (tq, window), 1))
    in_band = (c_chunk >= a_chunk) & (c_chunk - C_PAST_CHUNKS <= a_chunk)
    row = jnp.broadcast_to(rb_ref[hh:hh + 1, :], (tq, rb_ref.shape[1]))
    toeplitz = pltpu.roll(row, 0, 1, stride=1, stride_axis=0)[:, :window]
    return jnp.where(in_band, toeplitz, -jnp.inf)


def _c_attn_kernel(x_ref, mod_ref, q_ref, k_ref, vt_ref, rb_ref, wot_ref, o_ref, catt_ref, tbt_ref, *,
                   window, pos_base):
    _, tq, d = x_ref.shape
    v_chunk = vt_ref.shape[3]
    n_chunks = window // v_chunk
    half = C_HEAD_DIM

    @pl.when((pl.program_id(0) == 0) & (pl.program_id(1) == 0))
    def _():
        for hh in range(C_HEADS):
            tbt_ref[hh // 2, :, (hh % 2) * tq:(hh % 2 + 1) * tq] = _band_table(rb_ref, hh, tq, window).T

    j = pl.program_id(1)
    q0 = pl.multiple_of(j * tq, tq)
    rows = pl.ds(q0, window)
    chunk0 = j * (tq // v_chunk)
    lo64 = _lane_lo64((tq, LANES))
    zero = jnp.zeros((tq, LANES), BF16)
    visible = (pos_base + q0 + lax.broadcasted_iota(jnp.int32, (window, 2 * tq), 0)) >= 0
    for p in range(C_HEADS // 2):
        sl = slice(p * LANES, (p + 1) * LANES)
        qp = q_ref[0, :, sl]
        q2 = jnp.concatenate([jnp.where(lo64, qp, zero), jnp.where(lo64, zero, qp)], axis=0)
        s = _dot_t(k_ref[0, rows, sl], q2) + tbt_ref[p]
        s = jnp.where(visible, s, -jnp.inf)
        e = jnp.exp2(s - jnp.max(s, axis=0, keepdims=True))
        inv = 1.0 / jnp.sum(e, axis=0, keepdims=True)
        eb = e.astype(BF16)
        acc = _dot(vt_ref[0, chunk0, sl, :], eb[:v_chunk, :])
        for c in range(1, n_chunks):
            acc = acc + _dot(vt_ref[0, chunk0 + c, sl, :], eb[c * v_chunk:(c + 1) * v_chunk, :])
        o = (acc * inv).astype(BF16)
        catt_ref[p * LANES:p * LANES + half, :] = o[:half, :tq]
        catt_ref[p * LANES + half:(p + 1) * LANES, :] = o[half:, tq:]
    mix_t = _dot(wot_ref[...], catt_ref[...])
    o_ref[0] = x_ref[0] + mod_ref[0, 5] * mix_t.T


def _c_attn(x, mod, q, kpad, vt, rb, wot, *, tq, window, pos_base):
    bsz, lq, d = x.shape
    lkp = kpad.shape[1]
    hd = q.shape[2]
    assert tq % vt.shape[3] == 0 and window % vt.shape[3] == 0

    def qspec(n):
        return pl.BlockSpec((1, tq, n), lambda i, j: (i, j, 0))

    return pl.pallas_call(
        functools.partial(_c_attn_kernel, window=window, pos_base=pos_base),
        grid=(bsz, lq // tq),
        in_specs=[qspec(d), _mod_spec(1, d), qspec(hd),
                  pl.BlockSpec((1, lkp, hd), lambda i, j: (i, 0, 0)),
                  pl.BlockSpec((1,) + vt.shape[1:], lambda i, j: (i, 0, 0, 0)),
                  _vmem_spec(), _vmem_spec()],
        out_specs=qspec(d),
        out_shape=jax.ShapeDtypeStruct(x.shape, F32),
        scratch_shapes=[pltpu.VMEM((hd, tq), BF16),
                        pltpu.VMEM((C_HEADS // 2, window, 2 * tq), F32)],
        compiler_params=_params(2),
        name="c_attn",
    )(x, mod, q, kpad, vt, rb, wot)


def _c_attn_rows_kernel(x_ref, mod_ref, q_ref, k_ref, v_ref, rb_ref, wo_ref, o_ref, cat_ref, *,
                        window, lk_valid):
    _, tq, d = x_ref.shape
    visible = lax.broadcasted_iota(jnp.int32, (tq, window), 1) < lk_valid
    lo64 = _lane_lo64((tq, LANES))
    zero = jnp.zeros((tq, LANES), BF16)
    for p in range(C_HEADS // 2):
        sl = slice(p * LANES, (p + 1) * LANES)
        qp = q_ref[0, :, sl]
        kp = k_ref[0, :, sl]
        vp = v_ref[0, :, sl]
        halves = []
        for t in range(2):
            qt = jnp.where(lo64, qp, zero) if t == 0 else jnp.where(lo64, zero, qp)
            s = _dot_t(qt, kp) + _band_table(rb_ref, 2 * p + t, tq, window)
            s = jnp.where(visible, s, -jnp.inf)
            e = jnp.exp2(s - jnp.max(s, axis=-1, keepdims=True))
            inv = 1.0 / jnp.sum(e, axis=-1, keepdims=True)
            halves.append(_dot(e.astype(BF16), vp) * inv)
        cat_ref[:, sl] = jnp.where(lo64, halves[0], halves[1]).astype(BF16)
    mix = _dot(cat_ref[...], wo_ref[...])
    o_ref[0] = x_ref[0] + mod_ref[0, 5] * mix


def _c_attn_rows(x, mod, q, kpad, vpad, rb, wo, *, lk_valid):
    bsz, tq, d = x.shape
    window = kpad.shape[1]
    hd = q.shape[2]

    def spec(rows, n):
        return pl.BlockSpec((1, rows, n), lambda i, j: (i, 0, 0))

    return pl.pallas_call(
        functools.partial(_c_attn_rows_kernel, window=window, lk_valid=lk_valid),
        grid=(bsz, 1),
        in_specs=[spec(tq, d), _mod_spec(1, d), spec(tq, hd), spec(window, hd), spec(window, hd),
                  _vmem_spec(), _vmem_spec()],
        out_specs=spec(tq, d),
        out_shape=jax.ShapeDtypeStruct(x.shape, F32),
        scratch_shapes=[pltpu.VMEM((tq, hd), BF16)],
        compiler_params=_params(2),
        name="c_attn_sample",
    )(x, mod, q, kpad, vpad, rb, wo)


def _rope_tables(pos0, length):
    half = CHUNK // 2
    inv = 1.0 / (ROPE_THETA ** (jnp.arange(half, dtype=F32) * (2.0 / CHUNK)))
    ang = (pos0 + jnp.arange(length, dtype=jnp.int32)).astype(F32)[:, None] * inv[None, :]
    cos = jnp.cos(ang)
    sin = jnp.sin(ang)
    cos = jnp.concatenate([cos, cos, cos, cos], axis=-1)
    sin = jnp.concatenate([-sin, sin, -sin, sin], axis=-1)
    return cos, sin


def _band_bias_rows(rel_bias, tq, window):
    heads, n_rel = rel_bias.shape
    roll_w = _round_up(window + tq, LANES)
    lead = C_BAND_PAST - REL_CLIP
    n_mid = min(n_rel, window - lead)
    n_hi = window - lead - n_mid

    def rep(col, n):
        return jnp.broadcast_to(rel_bias[:, col:col + 1], (heads, n))

    rows = jnp.concatenate([rep(0, lead), rel_bias[:, :n_mid], rep(n_rel - 1, n_hi),
                            rep(0, roll_w - window)], axis=1)
    return rows.astype(F32) * LOG2E


def _pad_rows(a, rows):
    return jnp.pad(a, ((0, 0), (0, rows - a.shape[1]), (0, 0)))


def _round_up(n, m):
    return (n + m - 1) // m * m


def _prep_weights(ffn_w_in, ffn_w_out, ab_w_in, mla_q_norm_g, mla_w_uq, mla_kv_norm_g, mla_w_ukv,
                  ab_w_out, c_w_in, c_w_out):
    w = {"ffn_in": ffn_w_in.astype(BF16), "ffn_out": ffn_w_out.astype(BF16), "ab": [], "c": []}
    for i in range(ab_w_in.shape[0]):
        win = ab_w_in[i]
        pad = _round_up(win.shape[1], LANES) - win.shape[1]
        win = jnp.pad(win, ((0, 0), (0, pad))).astype(BF16)
        q_rank = mla_w_uq.shape[1]
        uq = mla_w_uq[i].reshape(q_rank, MLA_HEADS, MLA_NOPE + MLA_ROPE)
        uq_rope = jnp.pad(uq[:, :, MLA_NOPE:], ((0, 0), (0, 0), (0, LANES - MLA_ROPE)))
        wuq = jnp.concatenate([uq[:, :, :MLA_NOPE].reshape(q_rank, -1), uq_rope.reshape(q_rank, -1)], axis=1)
        kv_rank = mla_w_ukv.shape[1]
        ukv = mla_w_ukv[i].reshape(kv_rank, MLA_HEADS, MLA_NOPE + MLA_V)
        wo = ab_w_out[i].astype(BF16)
        w["ab"].append({
            "win": win,
            "qg": mla_q_norm_g[i].reshape(1, -1),
            "wuq": wuq.astype(BF16),
            "kvg": mla_kv_norm_g[i].reshape(1, -1),
            "wkn": ukv[:, :, :MLA_NOPE].reshape(kv_rank, -1).astype(BF16),
            "wv": ukv[:, :, MLA_NOPE:].reshape(kv_rank, -1).astype(BF16),
            "wo": wo, "wot": wo.T,
        })
    for i in range(c_w_in.shape[0]):
        wo = c_w_out[i].astype(BF16)
        w["c"].append({"win": c_w_in[i].astype(BF16), "wo": wo, "wot": wo.T})
    return w


def _trunk(x, mod, past, prm, w, *, prompt):
    bsz, length, d = x.shape
    depth = mod.shape[0]
    if prompt:
        nb, s = 1, min(ROW_TILE, length)
        pos0 = 0
    else:
        nb, s = bsz, length
        pos0 = past["a_k"].shape[2]
    cos, sin = _rope_tables(pos0, length)
    ab_rows, c_rows = [], []
    for l in range(depth):
        g = prm["norm_g"][l]
        x = _ffn(x, mod[l], g, w["ffn_in"][l, 0], w["ffn_out"][l, 0], k0=0, g_row=0, nb=nb, s=s)
        i = l // 2
        if l % 2 == 0:
            wl = w["ab"][i]
            lam_init = 0.8 - 0.6 * math.exp(-0.3 * l)
            sg = prm["a_subln_g"][i]
            if prompt:
                aq, qm, ak, av, lat, kr, akb, avt, km, vmt = _ab_proj(
                    x, mod[l], g, cos, sin, wl, nb=nb, s=s, kv_chunk=min(K_TILE_AB, length))
                x = _ab_attn(x, mod[l], aq, akb, avt, qm, km, vmt, prm["a_lambda"][i], sg.reshape(-1, 1),
                             wl["wot"], tq=min(Q_TILE_AB, length), q_pos0=0, lk_valid=length,
                             lam_init=lam_init)
            else:
                aq, qm, ak, av, lat, kr = _ab_proj(x, mod[l], g, cos, sin, wl, nb=nb, s=s)
                lk_valid = pos0 + length
                lk = _round_up(lk_valid, LANES)
                pa_k = past["a_k"][i].reshape(bsz, pos0, -1)
                pa_v = past["a_v"][i].reshape(bsz, pos0, -1)
                akb = _pad_rows(jnp.concatenate([pa_k, ak], axis=1).astype(BF16), lk)
                avb = _pad_rows(jnp.concatenate([pa_v, av], axis=1).astype(BF16), lk)
                lat_all = _pad_rows(jnp.concatenate([past["mla_latent"][i], lat], axis=1), lk)
                kr_all = jnp.concatenate([past["mla_krope"][i], kr], axis=1)
                kr_all = jnp.pad(kr_all, ((0, 0), (0, lk - lk_valid), (0, LANES - MLA_ROPE)))
                km, vm = _mla_kv_cache(lat_all, kr_all, wl)
                x = _ab_attn_rows(x, mod[l], aq, akb, avb, qm, km, vm, prm["a_lambda"][i], sg.reshape(1, -1),
                                  wl["wo"], q_pos0=pos0, lk_valid=lk_valid, lam_init=lam_init)
            ab_rows.append((ak.reshape(bsz, length, A_HEADS, 2, A_HEAD_DIM),
                            av.reshape(bsz, length, A_HEADS, A_V_DIM), lat, kr))
        else:
            wl = w["c"][i]
            hd = C_HEADS * C_HEAD_DIM
            if prompt:
                tq = min(Q_TILE_C, length)
                window = C_BAND_PAST + tq
                q, kpad, vt, ck, cv = _c_proj(x, mod[l], g, wl["win"], v_chunk=tq)
                rb = _band_bias_rows(prm["c_rel_bias"][i], tq, window)
                x = _c_attn(x, mod[l], q, kpad, vt, rb, wl["wot"], tq=tq, window=window,
                            pos_base=-C_BAND_PAST)
            else:
                pk = past["c_k"][i].reshape(bsz, -1, hd)
                pv = past["c_v"][i].reshape(bsz, -1, hd)
                lc = pk.shape[1]
                qkv = _modproj(x, mod[l], g, wl["win"])
                q = (qkv[:, :, :hd] * C_SCALE).astype(BF16)
                k_all = jnp.concatenate([pk, qkv[:, :, hd:2 * hd]], axis=1)
                v_all = jnp.concatenate([pv, qkv[:, :, 2 * hd:]], axis=1)
                window = _round_up(C_BAND_PAST + length, LANES)
                rb = _band_bias_rows(prm["c_rel_bias"][i], length, window)
                x = _c_attn_rows(x, mod[l], q, _pad_rows(k_all.astype(BF16), window),
                                 _pad_rows(v_all.astype(BF16), window), rb, wl["wo"], lk_valid=lc + length)
                ck, cv = k_all[:, -lc:], v_all[:, -lc:]
            c_rows.append((ck.reshape(bsz, -1, C_HEADS, C_HEAD_DIM), cv.reshape(bsz, -1, C_HEADS, C_HEAD_DIM)))
        x = _ffn(x, mod[l], g, w["ffn_in"][l, 1], w["ffn_out"][l, 1], k0=6, g_row=2, nb=nb, s=s,
                 final_g=prm["final_norm_g"] if l == depth - 1 else None)
    ab_state = tuple(jnp.stack([r[j] for r in ab_rows]) for j in range(4))
    c_state = tuple(jnp.stack([r[j] for r in c_rows]) for j in range(2))
    return x, ab_state, c_state


def kernel(x_prompt, x_sample, cache_a_k, cache_a_v, cache_mla_latent, cache_mla_krope, cache_c_k, cache_c_v,
           c_prompt, c_sample, ada_w, ada_b, norm_g, ffn_w_in, ffn_w_out, ab_w_in, a_lambda, a_subln_g,
           mla_q_norm_g, mla_w_uq, mla_kv_norm_g, mla_w_ukv, ab_w_out, c_w_in, c_rel_bias, c_w_out,
           final_norm_g):
    bp = x_prompt.shape[0]
    d = x_prompt.shape[2]
    depth = ada_w.shape[0]
    c_all = jnp.concatenate([c_prompt, c_sample], axis=0)
    n_seq = c_all.shape[0]
    c_all = jnp.pad(c_all, ((0, _round_up(n_seq, 16) - n_seq), (0, 0)))
    mod = _modulation(c_all, ada_w, ada_b)[:, :n_seq]
    mod = mod.reshape(depth, n_seq, N_MOD, 1, d)
    w = _prep_weights(ffn_w_in, ffn_w_out, ab_w_in, mla_q_norm_g, mla_w_uq, mla_kv_norm_g, mla_w_ukv,
                      ab_w_out, c_w_in, c_w_out)
    prm = dict(norm_g=norm_g, a_lambda=a_lambda, a_subln_g=a_subln_g, c_rel_bias=c_rel_bias,
               final_norm_g=final_norm_g)
    y_p, ab_p, c_p = _trunk(x_prompt, mod[:, :bp], None, prm, w, prompt=True)
    past = dict(a_k=cache_a_k, a_v=cache_a_v, mla_latent=cache_mla_latent, mla_krope=cache_mla_krope,
                c_k=cache_c_k, c_v=cache_c_v)
    y_s, ab_s, c_s = _trunk(x_sample, mod[:, bp:], past, prm, w, prompt=False)
    return (y_p, y_s, ab_p[0], ab_s[0], ab_p[1], ab_s[1], ab_p[2], ab_s[2], ab_p[3], ab_s[3],
            c_p[0], c_s[0], c_p[1], c_s[1])
```
